```python
import jax, jax.numpy as jnp
from jax import lax
import numpy as np

D_MODEL = 2048
BATCH = 2
SEQ = 8192
DEPTH = 1

D_MIX = D_MODEL
ATTN_WIDTH = D_MIX // 2
LRU_WIDTH = D_MIX - ATTN_WIDTH
ATTN_HEAD_DIM = 128
ATTN_HEADS = ATTN_WIDTH // ATTN_HEAD_DIM
LRU_BLOCKS = 8
LRU_BLOCK_W = LRU_WIDTH // LRU_BLOCKS
CONV_W = 4
LRU_C = 8.0
Q_BLOCK = 128
NORM_EPS = 1e-6
N_IN = 3 * ATTN_WIDTH + ATTN_HEADS + ATTN_WIDTH + LRU_WIDTH + LRU_WIDTH
SPLITS = (ATTN_WIDTH, 2 * ATTN_WIDTH, 3 * ATTN_WIDTH, 3 * ATTN_WIDTH + ATTN_HEADS,
          4 * ATTN_WIDTH + ATTN_HEADS, 4 * ATTN_WIDTH + ATTN_HEADS + LRU_WIDTH)

kernel_name = "hymba_fox_rglru_parallel_heads"


def _rmsnorm(x, g):
    x32 = x.astype(jnp.float32)
    y = x32 * lax.rsqrt(jnp.mean(x32 * x32, axis=-1, keepdims=True) + NORM_EPS)
    return (y * g.astype(jnp.float32)).astype(x.dtype)


def _forgetting_attention(q, k, v, log_f):
    B, S, H, Dh = q.shape
    qh = q.transpose(0, 2, 1, 3)
    kh = k.transpose(0, 2, 1, 3)
    vh = v.transpose(0, 2, 1, 3)
    c = jnp.cumsum(log_f, axis=1).transpose(0, 2, 1)
    kpos = jnp.arange(S)
    scale = ATTN_HEAD_DIM ** -0.5
    n_blocks = S // Q_BLOCK

    def block(i):
        start = i * Q_BLOCK
        qb = lax.dynamic_slice_in_dim(qh, start, Q_BLOCK, axis=2)
        cq = lax.dynamic_slice_in_dim(c, start, Q_BLOCK, axis=2)
        s = jnp.einsum('bhqd,bhkd->bhqk', qb, kh,
                       preferred_element_type=jnp.float32) * scale
        s = s + cq[..., :, None] - c[..., None, :]
        qpos = start + jnp.arange(Q_BLOCK)
        s = jnp.where(kpos[None, :] <= qpos[:, None], s, -jnp.inf)
        p = jax.nn.softmax(s, axis=-1)
        return jnp.einsum('bhqk,bhkd->bhqd', p.astype(vh.dtype), vh)

    out = lax.map(block, jnp.arange(n_blocks))
    return out.transpose(1, 0, 3, 2, 4).reshape(B, S, H * Dh)


def _causal_depthwise_conv(x, w, b):
    y = lax.conv_general_dilated(
        x, w[:, None, :].astype(x.dtype), window_strides=(1,),
        padding=[(CONV_W - 1, 0)], dimension_numbers=('NWC', 'WIO', 'NWC'),
        feature_group_count=x.shape[-1])
    return y + b.astype(x.dtype)


def _rg_lru(x, w_r, b_r, w_i, b_i, lam):
    B, S, C = x.shape
    xb = x.reshape(B, S, LRU_BLOCKS, LRU_BLOCK_W)
    r = jax.nn.sigmoid((jnp.einsum('bsnc,ncd->bsnd', xb, w_r).reshape(B, S, C) + b_r).astype(jnp.float32))
    i = jax.nn.sigmoid((jnp.einsum('bsnc,ncd->bsnd', xb, w_i).reshape(B, S, C) + b_i).astype(jnp.float32))
    log_a = -LRU_C * r * jax.nn.softplus(-lam.astype(jnp.float32))
    a = jnp.exp(log_a)
    mult = jnp.sqrt(-jnp.expm1(2.0 * log_a))
    u = mult * i * x.astype(jnp.float32)

    def combine(left, right):
        a1, b1 = left
        a2, b2 = right
        return a1 * a2, a2 * b1 + b2

    _, h = lax.associative_scan(combine, (a, u), axis=1)
    return h.astype(x.dtype)


def setup_inputs(seed: int = 0) -> dict:
    key = jax.random.key(seed)
    ks = jax.random.split(key, 16)
    f32 = jnp.float32
    x = jax.random.normal(ks[0], (BATCH, SEQ, D_MODEL), f32)
    norm_g = 1.0 + 0.01 * jax.random.normal(ks[1], (DEPTH, D_MODEL), f32)
    w_in = jax.random.normal(ks[2], (DEPTH, D_MODEL, N_IN), f32) * D_MODEL ** -0.5
    b_f = 2.0 + 0.5 * jax.random.normal(ks[3], (DEPTH, ATTN_HEADS), f32)
    conv_w = jax.random.normal(ks[4], (DEPTH, CONV_W, LRU_WIDTH), f32) * CONV_W ** -0.5
    conv_b = 0.01 * jax.random.normal(ks[5], (DEPTH, LRU_WIDTH), f32)
    w_rg = jax.random.normal(ks[6], (DEPTH, LRU_BLOCKS, LRU_BLOCK_W, LRU_BLOCK_W), f32) * LRU_BLOCK_W ** -0.5
    b_rg = 0.01 * jax.random.normal(ks[7], (DEPTH, LRU_WIDTH), f32)
    w_ig = jax.random.normal(ks[8], (DEPTH, LRU_BLOCKS, LRU_BLOCK_W, LRU_BLOCK_W), f32) * LRU_BLOCK_W ** -0.5
    b_ig = 0.01 * jax.random.normal(ks[9], (DEPTH, LRU_WIDTH), f32)
    a0 = jax.random.uniform(ks[10], (DEPTH, LRU_WIDTH), f32, minval=0.9, maxval=0.999)
    lru_lambda = jnp.log(a0) - jnp.log1p(-a0)
    attn_norm_g = 1.0 + 0.01 * jax.random.normal(ks[11], (DEPTH, ATTN_WIDTH), f32)
    lru_norm_g = 1.0 + 0.01 * jax.random.normal(ks[12], (DEPTH, LRU_WIDTH), f32)
    w_out = jax.random.normal(ks[13], (DEPTH, D_MIX, D_MODEL), f32) * D_MIX ** -0.5
    final_norm_g = 1.0 + 0.01 * jax.random.normal(ks[14], (D_MODEL,), f32)
    return {"x": x, "norm_g": norm_g, "w_in": w_in, "b_f": b_f, "conv_w": conv_w,
            "conv_b": conv_b, "w_rg": w_rg, "b_rg": b_rg, "w_ig": w_ig, "b_ig": b_ig,
            "lru_lambda": lru_lambda, "attn_norm_g": attn_norm_g, "lru_norm_g": lru_norm_g,
            "w_out": w_out, "final_norm_g": final_norm_g}


def reference(x, norm_g, w_in, b_f, conv_w, conv_b, w_rg, b_rg, w_ig, b_ig,
              lru_lambda, attn_norm_g, lru_norm_g, w_out, final_norm_g):
    B, S, _ = x.shape
    for l in range(DEPTH):
        h = _rmsnorm(x, norm_g[l])
        proj = jnp.einsum('bsd,dn->bsn', h, w_in[l])
        q, k, v, f_logit, z_attn, x_lru, z_lru = jnp.split(proj, SPLITS, axis=-1)
        log_f = jax.nn.log_sigmoid((f_logit + b_f[l]).astype(jnp.float32))
        attn = _forgetting_attention(q.reshape(B, S, ATTN_HEADS, ATTN_HEAD_DIM),
                                     k.reshape(B, S, ATTN_HEADS, ATTN_HEAD_DIM),
                                     v.reshape(B, S, ATTN_HEADS, ATTN_HEAD_DIM), log_f)
        attn = _rmsnorm(attn, attn_norm_g[l]) * jax.nn.silu(z_attn)
        u = _causal_depthwise_conv(x_lru, conv_w[l], conv_b[l])
        lru = _rg_lru(u, w_rg[l], b_rg[l], w_ig[l], b_ig[l], lru_lambda[l])
        lru = _rmsnorm(lru, lru_norm_g[l]) * jax.nn.silu(z_lru)
        mixed = jnp.concatenate([attn, lru], axis=-1)
        x = x + jnp.einsum('bsm,md->bsd', mixed, w_out[l])
    return _rmsnorm(x, final_norm_g)
```

```python
import functools

import jax
import jax.numpy as jnp
from jax import lax
from jax.experimental import pallas as pl
from jax.experimental.pallas import tpu as pltpu

ATTN_HEADS = 8
HEAD_DIM = 128
ATTN_WIDTH = ATTN_HEADS * HEAD_DIM
LRU_BLOCKS = 8
LRU_BLOCK_W = 128
LRU_WIDTH = LRU_BLOCKS * LRU_BLOCK_W
CONV_W = 4
LRU_C = 8.0
NORM_EPS = 1e-6
N_MAIN = 3 * ATTN_WIDTH + ATTN_WIDTH + 2 * LRU_WIDTH

SUBLANES_F32 = 8
HEAD_ROWS = 16
VMEM_LIMIT_BYTES = 58 * 1024 * 1024

TM_IN = 512
TN_IN = 1024
CUM_CHUNK = 256
TQ = 512
TK = 512
T_LRU = 256
TM_OUT = 512

_NEG_BIG = -1e30


def _sigmoid(x):
    return 1.0 / (1.0 + jnp.exp(-x))


def _log_sigmoid(x):
    return jnp.minimum(x, 0.0) - jnp.log1p(jnp.exp(-jnp.abs(x)))


def _softplus(x):
    return jnp.maximum(x, 0.0) + jnp.log1p(jnp.exp(-jnp.abs(x)))


def _rms_scale(x32):
    return lax.rsqrt(jnp.mean(x32 * x32, axis=-1, keepdims=True) + NORM_EPS)


def _in_proj_kernel(x_ref, g_ref, w_ref, cs_ref, wf_ref, bf_ref, proj_ref, c_ref, carry_ref,
                    *, steps_per_seq):
    i = pl.program_id(0)
    x = x_ref[...]
    h = (x * _rms_scale(x) * g_ref[...]).astype(jnp.bfloat16)

    for n in range(N_MAIN // TN_IN):
        cols = slice(n * TN_IN, (n + 1) * TN_IN)
        acc = jnp.dot(h, w_ref[:, cols], preferred_element_type=jnp.float32)
        proj_ref[:, cols] = (acc * cs_ref[:, cols]).astype(proj_ref.dtype)

    f = lax.dot_general(wf_ref[...], h, (((1,), (1,)), ((), ())),
                        preferred_element_type=jnp.float32)
    logf = _log_sigmoid(f + bf_ref[...])

    @pl.when(i % steps_per_seq == 0)
    def _():
        carry_ref[...] = jnp.zeros_like(carry_ref)

    r = lax.broadcasted_iota(jnp.int32, (CUM_CHUNK, CUM_CHUNK), 0)
    c = lax.broadcasted_iota(jnp.int32, (CUM_CHUNK, CUM_CHUNK), 1)
    tri = jnp.where(r <= c, 1.0, 0.0).astype(jnp.bfloat16)
    carry = carry_ref[:, 0:1]
    for n in range(TM_IN // CUM_CHUNK):
        cols = slice(n * CUM_CHUNK, (n + 1) * CUM_CHUNK)
        v = logf[:, cols]
        hi = v.astype(jnp.bfloat16)
        r1 = v - hi.astype(jnp.float32)
        mid = r1.astype(jnp.bfloat16)
        lo = (r1 - mid.astype(jnp.float32)).astype(jnp.bfloat16)
        cs = (jnp.dot(hi, tri, preferred_element_type=jnp.float32)
              + jnp.dot(mid, tri, preferred_element_type=jnp.float32)
              + jnp.dot(lo, tri, preferred_element_type=jnp.float32)) + carry
        c_ref[:, cols] = cs
        carry = cs[:, CUM_CHUNK - 1:CUM_CHUNK]
    carry_ref[...] = jnp.broadcast_to(carry, carry_ref.shape)


def _in_proj(x2, g, w_main, colscale, w_ft, b_f, seq_len):
    m, d = x2.shape
    kern = functools.partial(_in_proj_kernel, steps_per_seq=seq_len // TM_IN)
    return pl.pallas_call(
        kern,
        grid=(m // TM_IN,),
        in_specs=[
            pl.BlockSpec((TM_IN, d), lambda i: (i, 0)),
            pl.BlockSpec((1, d), lambda i: (0, 0)),
            pl.BlockSpec((d, N_MAIN), lambda i: (0, 0), pipeline_mode=pl.Buffered(1)),
            pl.BlockSpec((1, N_MAIN), lambda i: (0, 0)),
            pl.BlockSpec((HEAD_ROWS, d), lambda i: (0, 0)),
            pl.BlockSpec((HEAD_ROWS, 1), lambda i: (0, 0)),
        ],
        out_specs=[
            pl.BlockSpec((TM_IN, N_MAIN), lambda i: (i, 0)),
            pl.BlockSpec((HEAD_ROWS, TM_IN), lambda i: (0, i)),
        ],
        out_shape=[
            jax.ShapeDtypeStruct((m, N_MAIN), jnp.bfloat16),
            jax.ShapeDtypeStruct((HEAD_ROWS, m), jnp.float32),
        ],
        scratch_shapes=[pltpu.VMEM((HEAD_ROWS, 128), jnp.float32)],
        compiler_params=pltpu.CompilerParams(
            dimension_semantics=("arbitrary",), vmem_limit_bytes=VMEM_LIMIT_BYTES),
        name="in_proj",
    )(x2, g, w_main, colscale, w_ft, b_f)


def _attn_kernel(q_ref, k_ref, v_ref, c_ref, o_ref, m_ref, l_ref, acc_ref):
    h = pl.program_id(1)
    qi = pl.program_id(2)
    q = q_ref[...]

    m_ref[...] = jnp.full_like(m_ref, _NEG_BIG)
    l_ref[...] = jnp.zeros_like(l_ref)
    acc_ref[...] = jnp.zeros_like(acc_ref)

    def block(k0, masked):
        kb = k_ref[pl.ds(k0, TK), :]
        s = lax.dot_general(q, kb, (((1,), (1,)), ((), ())),
                            preferred_element_type=jnp.float32)
        s = s - c_ref[pl.ds(h, 1), pl.ds(k0, TK)]
        if masked:
            row = lax.broadcasted_iota(jnp.int32, (TQ, TK), 0)
            col = lax.broadcasted_iota(jnp.int32, (TQ, TK), 1)
            s = jnp.where(col <= row, s, _NEG_BIG)
        m_prev = m_ref[...]
        m_new = jnp.maximum(m_prev, jnp.max(s, axis=1, keepdims=True))
        alpha = jnp.exp(m_prev - m_new)
        p = jnp.exp(s - m_new)
        l_ref[...] = alpha * l_ref[...] + jnp.sum(p, axis=1, keepdims=True)
        acc_ref[...] = alpha * acc_ref[...] + jnp.dot(
            p.astype(jnp.bfloat16), v_ref[pl.ds(k0, TK), :], preferred_element_type=jnp.float32)
        m_ref[...] = m_new

    def body(kj, carry):
        block(pl.multiple_of(kj * TK, TK), masked=False)
        return carry

    lax.fori_loop(0, qi, body, 0)
    block(pl.multiple_of(qi * TK, TK), masked=True)
    o_ref[...] = (acc_ref[...] / l_ref[...]).astype(o_ref.dtype)


def _attention(proj, c_t, batch, seq_len):
    assert TQ == TK
    m = proj.shape[0]
    nq = seq_len // TQ
    kv_off = ATTN_WIDTH // HEAD_DIM
    return pl.pallas_call(
        _attn_kernel,
        grid=(batch, ATTN_HEADS, nq),
        in_specs=[
            pl.BlockSpec((TQ, HEAD_DIM), lambda b, h, qi: (b * nq + qi, h)),
            pl.BlockSpec((seq_len, HEAD_DIM), lambda b, h, qi: (b, kv_off + h)),
            pl.BlockSpec((seq_len, HEAD_DIM), lambda b, h, qi: (b, 2 * kv_off + h)),
            pl.BlockSpec((HEAD_ROWS, seq_len), lambda b, h, qi: (0, b)),
        ],
        out_specs=pl.BlockSpec((TQ, HEAD_DIM), lambda b, h, qi: (b * nq + qi, h)),
        out_shape=jax.ShapeDtypeStruct((m, ATTN_WIDTH), jnp.bfloat16),
        scratch_shapes=[
            pltpu.VMEM((TQ, 1), jnp.float32),
            pltpu.VMEM((TQ, 1), jnp.float32),
            pltpu.VMEM((TQ, HEAD_DIM), jnp.float32),
        ],
        compiler_params=pltpu.CompilerParams(
            dimension_semantics=("arbitrary", "arbitrary", "arbitrary"),
            vmem_limit_bytes=VMEM_LIMIT_BYTES),
        name="fox_attn",
    )(proj, proj, proj, c_t)


def _rglru_kernel(x_ref, z_ref, cw_ref, cb_ref, wg_ref, br_ref, bi_ref, lam_ref, g_ref,
                  o_ref, xs_ref, a_ref, b_ref, h_ref, hc_ref):
    t = pl.program_id(1)
    halo = SUBLANES_F32

    @pl.when(t == 0)
    def _():
        xs_ref[0:halo, :] = jnp.zeros((halo, LRU_WIDTH), jnp.float32)
        hc_ref[...] = jnp.zeros_like(hc_ref)

    xs_ref[halo:halo + T_LRU, :] = x_ref[...].astype(jnp.float32)
    u = cb_ref[...] + cw_ref[CONV_W - 1:CONV_W, :] * xs_ref[halo:halo + T_LRU, :]
    for k in range(CONV_W - 1):
        d = CONV_W - 1 - k
        u = u + cw_ref[k:k + 1, :] * xs_ref[halo - d:halo - d + T_LRU, :]
    xs_ref[0:halo, :] = xs_ref[T_LRU:T_LRU + halo, :]

    sp = _softplus(-lam_ref[...])
    ub = u.astype(jnp.bfloat16)
    for n in range(LRU_BLOCKS):
        cols = slice(n * LRU_BLOCK_W, (n + 1) * LRU_BLOCK_W)
        gate = jnp.dot(ub[:, cols], wg_ref[n], preferred_element_type=jnp.float32)
        r = _sigmoid(gate[:, :LRU_BLOCK_W] + br_ref[:, cols])
        ig = _sigmoid(gate[:, LRU_BLOCK_W:] + bi_ref[:, cols])
        log_a = (-LRU_C) * r * sp[:, cols]
        a = jnp.exp(log_a)
        a_ref[:, cols] = a
        one_minus_a2 = jnp.tanh(-log_a) * (1.0 + a * a)
        b_ref[:, cols] = jnp.sqrt(one_minus_a2) * ig * u[:, cols]

    row = lax.broadcasted_iota(jnp.int32, (SUBLANES_F32, LRU_WIDTH), 0)

    def group(gidx, hprev):
        r0 = pl.multiple_of(gidx * SUBLANES_F32, SUBLANES_F32)
        av = a_ref[pl.ds(r0, SUBLANES_F32), :]
        bv = b_ref[pl.ds(r0, SUBLANES_F32), :]
        for d in (1, 2, 4):
            a_sh = jnp.where(row >= d, pltpu.roll(av, d, 0), 1.0)
            b_sh = jnp.where(row >= d, pltpu.roll(bv, d, 0), 0.0)
            bv = av * b_sh + bv
            av = av * a_sh
        hv = av * hprev + bv
        h_ref[pl.ds(r0, SUBLANES_F32), :] = hv
        return hv[SUBLANES_F32 - 1:SUBLANES_F32, :]

    hc_ref[0:1, :] = lax.fori_loop(0, T_LRU // SUBLANES_F32, group, hc_ref[0:1, :])

    hv = h_ref[...]
    z = z_ref[...].astype(jnp.float32)
    y = hv * _rms_scale(hv) * g_ref[...]
    o_ref[...] = (y * (z * _sigmoid(z))).astype(o_ref.dtype)


def _rglru(proj, conv_w, conv_b, w_gate, b_rg, b_ig, lam, g, batch, seq_len):
    m = proj.shape[0]
    nt = seq_len // T_LRU
    x_blk = (3 * ATTN_WIDTH + ATTN_WIDTH) // LRU_WIDTH
    vec = lambda: pl.BlockSpec((1, LRU_WIDTH), lambda b, t: (0, 0))
    return pl.pallas_call(
        _rglru_kernel,
        grid=(batch, nt),
        in_specs=[
            pl.BlockSpec((T_LRU, LRU_WIDTH), lambda b, t: (b * nt + t, x_blk)),
            pl.BlockSpec((T_LRU, LRU_WIDTH), lambda b, t: (b * nt + t, x_blk + 1)),
            pl.BlockSpec((CONV_W, LRU_WIDTH), lambda b, t: (0, 0)),
            vec(),
            pl.BlockSpec((LRU_BLOCKS, LRU_BLOCK_W, 2 * LRU_BLOCK_W), lambda b, t: (0, 0, 0)),
            vec(), vec(), vec(), vec(),
        ],
        out_specs=pl.BlockSpec((T_LRU, LRU_WIDTH), lambda b, t: (b * nt + t, 0)),
        out_shape=jax.ShapeDtypeStruct((m, LRU_WIDTH), jnp.bfloat16),
        scratch_shapes=[
            pltpu.VMEM((T_LRU + SUBLANES_F32, LRU_WIDTH), jnp.float32),
            pltpu.VMEM((T_LRU, LRU_WIDTH), jnp.float32),
            pltpu.VMEM((T_LRU, LRU_WIDTH), jnp.float32),
            pltpu.VMEM((T_LRU, LRU_WIDTH), jnp.float32),
            pltpu.VMEM((SUBLANES_F32, LRU_WIDTH), jnp.float32),
        ],
        compiler_params=pltpu.CompilerParams(
            dimension_semantics=("arbitrary", "arbitrary"), vmem_limit_bytes=VMEM_LIMIT_BYTES),
        name="rglru",
    )(proj, proj, conv_w, conv_b, w_gate, b_rg, b_ig, lam, g)


def _out_proj_kernel(x_ref, a_ref, z_ref, l_ref, ga_ref, w_ref, gf_ref, o_ref):
    a = a_ref[...].astype(jnp.float32)
    z = z_ref[...].astype(jnp.float32)
    ma = (a * _rms_scale(a) * ga_ref[...] * (z * _sigmoid(z))).astype(jnp.bfloat16)
    o = (x_ref[...]
         + jnp.dot(ma, w_ref[0:ATTN_WIDTH, :], preferred_element_type=jnp.float32)
         + jnp.dot(l_ref[...], w_ref[ATTN_WIDTH:, :], preferred_element_type=jnp.float32))
    o_ref[...] = o * _rms_scale(o) * gf_ref[...]


def _out_proj(x2, attn, proj, mixed_lru, g_attn, w_out, g_final):
    m, d = x2.shape
    z_blk = 3 * ATTN_WIDTH // ATTN_WIDTH
    return pl.pallas_call(
        _out_proj_kernel,
        grid=(m // TM_OUT,),
        in_specs=[
            pl.BlockSpec((TM_OUT, d), lambda i: (i, 0)),
            pl.BlockSpec((TM_OUT, ATTN_WIDTH), lambda i: (i, 0)),
            pl.BlockSpec((TM_OUT, ATTN_WIDTH), lambda i: (i, z_blk)),
            pl.BlockSpec((TM_OUT, LRU_WIDTH), lambda i: (i, 0)),
            pl.BlockSpec((1, ATTN_WIDTH), lambda i: (0, 0)),
            pl.BlockSpec((ATTN_WIDTH + LRU_WIDTH, d), lambda i: (0, 0), pipeline_mode=pl.Buffered(1)),
            pl.BlockSpec((1, d), lambda i: (0, 0)),
        ],
        out_specs=pl.BlockSpec((TM_OUT, d), lambda i: (i, 0)),
        out_shape=jax.ShapeDtypeStruct((m, d), jnp.float32),
        compiler_params=pltpu.CompilerParams(
            dimension_semantics=("arbitrary",), vmem_limit_bytes=VMEM_LIMIT_BYTES),
        name="out_proj",
    )(x2, attn, proj, mixed_lru, g_attn, w_out, g_final)


def kernel(x, norm_g, w_in, b_f, conv_w, conv_b, w_rg, b_rg, w_ig, b_ig, lru_lambda,
           attn_norm_g, lru_norm_g, w_out, final_norm_g):
    batch, seq_len, d = x.shape
    assert norm_g.shape[0] == 1, "single-layer trunk"
    assert seq_len % TM_IN == 0 and seq_len % TQ == 0 and seq_len % T_LRU == 0
    f_lo = 3 * ATTN_WIDTH
    f_hi = f_lo + ATTN_HEADS
    bf16 = jnp.bfloat16

    w0 = w_in[0]
    w_main = jnp.concatenate([w0[:, :f_lo], w0[:, f_hi:]], axis=1).astype(bf16)
    w_ft = jnp.pad(w0[:, f_lo:f_hi].T, ((0, HEAD_ROWS - ATTN_HEADS), (0, 0))).astype(bf16)
    bf_col = jnp.pad(b_f[0], (0, HEAD_ROWS - ATTN_HEADS)).reshape(HEAD_ROWS, 1)
    colscale = jnp.concatenate([jnp.full((1, ATTN_WIDTH), HEAD_DIM ** -0.5, jnp.float32),
                                jnp.ones((1, N_MAIN - ATTN_WIDTH), jnp.float32)], axis=1)
    w_gate = jnp.concatenate([w_rg[0], w_ig[0]], axis=-1).astype(bf16)
    row = lambda v: v.reshape(1, -1)

    x2 = x.reshape(batch * seq_len, d)
    proj, c_t = _in_proj(x2, row(norm_g[0]), w_main, colscale, w_ft, bf_col, seq_len)
    attn = _attention(proj, c_t, batch, seq_len)
    mixed_lru = _rglru(proj, conv_w[0], row(conv_b[0]), w_gate, row(b_rg[0]), row(b_ig[0]),
                       row(lru_lambda[0]), row(lru_norm_g[0]), batch, seq_len)
    out = _out_proj(x2, attn, proj, mixed_lru, row(attn_norm_g[0]), w_out[0].astype(bf16),
                    row(final_norm_g))
    return out.reshape(batch, seq_len, d)
```

```python
import functools

import jax
import jax.numpy as jnp
from jax import lax
from jax.experimental import pallas as pl
from jax.experimental.pallas import tpu as pltpu

ATTN_HEADS = 8
HEAD_DIM = 128
ATTN_WIDTH = ATTN_HEADS * HEAD_DIM
LRU_BLOCKS = 8
LRU_BLOCK_W = 128
LRU_WIDTH = LRU_BLOCKS * LRU_BLOCK_W
CONV_W = 4
LRU_C = 8.0
NORM_EPS = 1e-6
LOG2_E = 1.4426950408889634
N_MAIN = 3 * ATTN_WIDTH + ATTN_WIDTH + 2 * LRU_WIDTH

SUBLANES_F32 = 8
HEAD_ROWS = 16
VMEM_LIMIT_BYTES = 58 * 1024 * 1024

TM_IN = 512
TN_IN = 1024
CUM_CHUNK = 256
TQ = 1024
TK = 512
T_LRU = 256
TM_OUT = 512

_NEG_BIG = -1e30


def _sigmoid(x):
    return 0.5 + 0.5 * jnp.tanh(0.5 * x)


def _log_sigmoid(x):
    return jnp.minimum(x, 0.0) - jnp.log1p(jnp.exp(-jnp.abs(x)))


def _softplus(x):
    return jnp.maximum(x, 0.0) + jnp.log1p(jnp.exp(-jnp.abs(x)))


def _rms_scale(x32):
    return lax.rsqrt(jnp.mean(x32 * x32, axis=-1, keepdims=True) + NORM_EPS)


def _in_proj_kernel(x_ref, g_ref, w_ref, cs_ref, wf_ref, bf_ref, proj_ref, c_ref, carry_ref,
                    *, steps_per_seq):
    i = pl.program_id(0)
    x = x_ref[...]
    h = (x * _rms_scale(x) * g_ref[...]).astype(jnp.bfloat16)

    for n in range(N_MAIN // TN_IN):
        cols = slice(n * TN_IN, (n + 1) * TN_IN)
        acc = jnp.dot(h, w_ref[:, cols], preferred_element_type=jnp.float32)
        proj_ref[:, cols] = (acc * cs_ref[:, cols]).astype(proj_ref.dtype)

    f = lax.dot_general(wf_ref[...], h, (((1,), (1,)), ((), ())),
                        preferred_element_type=jnp.float32)
    logf = _log_sigmoid(f + bf_ref[...]) * LOG2_E

    @pl.when(i % steps_per_seq == 0)
    def _():
        carry_ref[...] = jnp.zeros_like(carry_ref)

    r = lax.broadcasted_iota(jnp.int32, (CUM_CHUNK, CUM_CHUNK), 0)
    c = lax.broadcasted_iota(jnp.int32, (CUM_CHUNK, CUM_CHUNK), 1)
    tri = jnp.where(r <= c, 1.0, 0.0).astype(jnp.bfloat16)
    carry = carry_ref[:, 0:1]
    for n in range(TM_IN // CUM_CHUNK):
        cols = slice(n * CUM_CHUNK, (n + 1) * CUM_CHUNK)
        v = logf[:, cols]
        hi = v.astype(jnp.bfloat16)
        r1 = v - hi.astype(jnp.float32)
        mid = r1.astype(jnp.bfloat16)
        lo = (r1 - mid.astype(jnp.float32)).astype(jnp.bfloat16)
        cs = (jnp.dot(hi, tri, preferred_element_type=jnp.float32)
              + jnp.dot(mid, tri, preferred_element_type=jnp.float32)
              + jnp.dot(lo, tri, preferred_element_type=jnp.float32)) + carry
        c_ref[:, cols] = cs
        carry = cs[:, CUM_CHUNK - 1:CUM_CHUNK]
    carry_ref[...] = jnp.broadcast_to(carry, carry_ref.shape)


def _in_proj(x2, g, w_main, colscale, w_ft, b_f, seq_len):
    m, d = x2.shape
    kern = functools.partial(_in_proj_kernel, steps_per_seq=seq_len // TM_IN)
    return pl.pallas_call(
        kern,
        grid=(m // TM_IN,),
        in_specs=[
            pl.BlockSpec((TM_IN, d), lambda i: (i, 0)),
            pl.BlockSpec((1, d), lambda i: (0, 0)),
            pl.BlockSpec((d, N_MAIN), lambda i: (0, 0), pipeline_mode=pl.Buffered(1)),
            pl.BlockSpec((1, N_MAIN), lambda i: (0, 0)),
            pl.BlockSpec((HEAD_ROWS, d), lambda i: (0, 0)),
            pl.BlockSpec((HEAD_ROWS, 1), lambda i: (0, 0)),
        ],
        out_specs=[
            pl.BlockSpec((TM_IN, N_MAIN), lambda i: (i, 0)),
            pl.BlockSpec((HEAD_ROWS, TM_IN), lambda i: (0, i)),
        ],
        out_shape=[
            jax.ShapeDtypeStruct((m, N_MAIN), jnp.bfloat16),
            jax.ShapeDtypeStruct((HEAD_ROWS, m), jnp.float32),
        ],
        scratch_shapes=[pltpu.VMEM((HEAD_ROWS, 128), jnp.float32)],
        compiler_params=pltpu.CompilerParams(
            dimension_semantics=("arbitrary",), vmem_limit_bytes=VMEM_LIMIT_BYTES),
        name="in_proj",
    )(x2, g, w_main, colscale, w_ft, b_f)


def _attn_kernel(q_ref, k_ref, v_ref, c_ref, o_ref, vx_ref, s_ref, m_ref, acc_ref):
    h = pl.program_id(1)
    qi = pl.program_id(2)
    lanes = HEAD_DIM
    sub = TQ // TK

    @pl.when(qi == 0)
    def _():
        vx_ref[:, 0:lanes] = v_ref[...]
        vx_ref[:, lanes:] = jnp.ones((vx_ref.shape[0], lanes), vx_ref.dtype)

    q = q_ref[...]
    m_ref[...] = jnp.full_like(m_ref, _NEG_BIG)
    acc_ref[...] = jnp.zeros_like(acc_ref)

    def scores(c):
        k0 = pl.multiple_of(c * TK, TK)
        s = lax.dot_general(q, k_ref[pl.ds(k0, TK), :], (((1,), (1,)), ((), ())),
                            preferred_element_type=jnp.float32)
        return s - c_ref[pl.ds(h, 1), pl.ds(k0, TK)]

    def step(c, slot, band_chunk, has_next):
        s = s_ref[slot]
        if has_next:
            s_ref[1 - slot] = scores(c + 1)
        if band_chunk is not None:
            row = lax.broadcasted_iota(jnp.int32, (TQ, TK), 0)
            col = lax.broadcasted_iota(jnp.int32, (TQ, TK), 1) + band_chunk * TK
            s = jnp.where(col <= row, s, _NEG_BIG)
        tiles = [s[:, i * lanes:(i + 1) * lanes] for i in range(TK // lanes)]
        m_cur = functools.reduce(jnp.maximum, tiles)
        m_prev = m_ref[...]
        m_new = jnp.maximum(m_prev, jnp.max(m_cur, axis=1, keepdims=True))
        alpha = jnp.exp2(m_prev - m_new)
        p = jnp.concatenate([jnp.exp2(t - m_new) for t in tiles], axis=1).astype(jnp.bfloat16)
        k0 = pl.multiple_of(c * TK, TK)
        pv = jnp.dot(p, vx_ref[pl.ds(k0, TK), :], preferred_element_type=jnp.float32)
        acc_ref[:, 0:lanes] = alpha * acc_ref[:, 0:lanes] + pv[:, 0:lanes]
        acc_ref[:, lanes:] = alpha * acc_ref[:, lanes:] + pv[:, lanes:]
        m_ref[...] = m_new

    s_ref[0] = scores(0)

    def body(j, carry):
        for si in range(sub):
            step(j * sub + si, si % 2, None, True)
        return carry

    lax.fori_loop(0, qi, body, 0)
    for si in range(sub):
        step(qi * sub + si, si % 2, si, si + 1 < sub)
    o_ref[...] = (acc_ref[:, 0:lanes] / acc_ref[:, lanes:]).astype(o_ref.dtype)


def _attention(proj, c_t, batch, seq_len):
    assert TQ % TK == 0 and (TQ // TK) % 2 == 0
    m = proj.shape[0]
    nq = seq_len // TQ
    kv_off = ATTN_WIDTH // HEAD_DIM
    return pl.pallas_call(
        _attn_kernel,
        grid=(batch, ATTN_HEADS, nq),
        in_specs=[
            pl.BlockSpec((TQ, HEAD_DIM), lambda b, h, qi: (b * nq + qi, h)),
            pl.BlockSpec((seq_len, HEAD_DIM), lambda b, h, qi: (b, kv_off + h)),
            pl.BlockSpec((seq_len, HEAD_DIM), lambda b, h, qi: (b, 2 * kv_off + h)),
            pl.BlockSpec((HEAD_ROWS, seq_len), lambda b, h, qi: (0, b)),
        ],
        out_specs=pl.BlockSpec((TQ, HEAD_DIM), lambda b, h, qi: (b * nq + qi, h)),
        out_shape=jax.ShapeDtypeStruct((m, ATTN_WIDTH), jnp.bfloat16),
        scratch_shapes=[
            pltpu.VMEM((seq_len, 2 * HEAD_DIM), jnp.bfloat16),
            pltpu.VMEM((2, TQ, TK), jnp.float32),
            pltpu.VMEM((TQ, HEAD_DIM), jnp.float32),
            pltpu.VMEM((TQ, 2 * HEAD_DIM), jnp.float32),
        ],
        compiler_params=pltpu.CompilerParams(
            dimension_semantics=("arbitrary", "arbitrary", "arbitrary"),
            vmem_limit_bytes=VMEM_LIMIT_BYTES),
        name="fox_attn",
    )(proj, proj, proj, c_t)


def _rglru_kernel(x_ref, z_ref, cw_ref, cb_ref, wg_ref, br_ref, bi_ref, lam_ref, g_ref,
                  o_ref, xs_ref, a_ref, b_ref, h_ref, hc_ref):
    t = pl.program_id(1)
    halo = SUBLANES_F32

    @pl.when(t == 0)
    def _():
        xs_ref[0:halo, :] = jnp.zeros((halo, LRU_WIDTH), jnp.float32)
        hc_ref[...] = jnp.zeros_like(hc_ref)

    xs_ref[halo:halo + T_LRU, :] = x_ref[...].astype(jnp.float32)
    u = cb_ref[...] + cw_ref[CONV_W - 1:CONV_W, :] * xs_ref[halo:halo + T_LRU, :]
    for k in range(CONV_W - 1):
        d = CONV_W - 1 - k
        u = u + cw_ref[k:k + 1, :] * xs_ref[halo - d:halo - d + T_LRU, :]
    xs_ref[0:halo, :] = xs_ref[T_LRU:T_LRU + halo, :]

    decay = LRU_C * _softplus(-lam_ref[...])
    decay2 = -LOG2_E * decay
    ub = u.astype(jnp.bfloat16)
    for n in range(LRU_BLOCKS):
        cols = slice(n * LRU_BLOCK_W, (n + 1) * LRU_BLOCK_W)
        gate = jnp.dot(ub[:, cols], wg_ref[n], preferred_element_type=jnp.float32)
        r = _sigmoid(gate[:, :LRU_BLOCK_W] + br_ref[:, cols])
        ig = _sigmoid(gate[:, LRU_BLOCK_W:] + bi_ref[:, cols])
        a = jnp.exp2(r * decay2[:, cols])
        a_ref[:, cols] = a
        one_minus_a2 = jnp.tanh(r * decay[:, cols]) * (1.0 + a * a)
        pos = one_minus_a2 > 0.0
        mult = jnp.where(pos, one_minus_a2 * lax.rsqrt(jnp.where(pos, one_minus_a2, 1.0)), 0.0)
        b_ref[:, cols] = mult * ig * u[:, cols]

    row = lax.broadcasted_iota(jnp.int32, (SUBLANES_F32, LRU_WIDTH), 0)

    def group(gidx, hprev):
        r0 = pl.multiple_of(gidx * SUBLANES_F32, SUBLANES_F32)
        av = a_ref[pl.ds(r0, SUBLANES_F32), :]
        bv = b_ref[pl.ds(r0, SUBLANES_F32), :]
        for d in (1, 2, 4):
            a_sh = jnp.where(row >= d, pltpu.roll(av, d, 0), 1.0)
            b_sh = jnp.where(row >= d, pltpu.roll(bv, d, 0), 0.0)
            bv = av * b_sh + bv
            av = av * a_sh
        hv = av * hprev + bv
        h_ref[pl.ds(r0, SUBLANES_F32), :] = hv
        return hv[SUBLANES_F32 - 1:SUBLANES_F32, :]

    hc_ref[0:1, :] = lax.fori_loop(0, T_LRU // SUBLANES_F32, group, hc_ref[0:1, :])

    hv = h_ref[...]
    z = z_ref[...].astype(jnp.float32)
    y = hv * _rms_scale(hv) * g_ref[...]
    o_ref[...] = (y * (z * _sigmoid(z))).astype(o_ref.dtype)


def _rglru(proj, conv_w, conv_b, w_gate, b_rg, b_ig, lam, g, batch, seq_len):
    m = proj.shape[0]
    nt = seq_len // T_LRU
    x_blk = (3 * ATTN_WIDTH + ATTN_WIDTH) // LRU_WIDTH
    vec = lambda: pl.BlockSpec((1, LRU_WIDTH), lambda b, t: (0, 0))
    return pl.pallas_call(
        _rglru_kernel,
        grid=(batch, nt),
        in_specs=[
            pl.BlockSpec((T_LRU, LRU_WIDTH), lambda b, t: (b * nt + t, x_blk)),
            pl.BlockSpec((T_LRU, LRU_WIDTH), lambda b, t: (b * nt + t, x_blk + 1)),
            pl.BlockSpec((CONV_W, LRU_WIDTH), lambda b, t: (0, 0)),
            vec(),
            pl.BlockSpec((LRU_BLOCKS, LRU_BLOCK_W, 2 * LRU_BLOCK_W), lambda b, t: (0, 0, 0)),
            vec(), vec(), vec(), vec(),
        ],
        out_specs=pl.BlockSpec((T_LRU, LRU_WIDTH), lambda b, t: (b * nt + t, 0)),
        out_shape=jax.ShapeDtypeStruct((m, LRU_WIDTH), jnp.bfloat16),
        scratch_shapes=[
            pltpu.VMEM((T_LRU + SUBLANES_F32, LRU_WIDTH), jnp.float32),
            pltpu.VMEM((T_LRU, LRU_WIDTH), jnp.float32),
            pltpu.VMEM((T_LRU, LRU_WIDTH), jnp.float32),
            pltpu.VMEM((T_LRU, LRU_WIDTH), jnp.float32),
            pltpu.VMEM((SUBLANES_F32, LRU_WIDTH), jnp.float32),
        ],
        compiler_params=pltpu.CompilerParams(
            dimension_semantics=("arbitrary", "arbitrary"), vmem_limit_bytes=VMEM_LIMIT_BYTES),
        name="rglru",
    )(proj, proj, conv_w, conv_b, w_gate, b_rg, b_ig, lam, g)


def _out_proj_kernel(x_ref, a_ref, z_ref, l_ref, ga_ref, w_ref, gf_ref, o_ref):
    a = a_ref[...].astype(jnp.float32)
    z = z_ref[...].astype(jnp.float32)
    ma = (a * _rms_scale(a) * ga_ref[...] * (z * _sigmoid(z))).astype(jnp.bfloat16)
    o = (x_ref[...]
         + jnp.dot(ma, w_ref[0:ATTN_WIDTH, :], preferred_element_type=jnp.float32)
         + jnp.dot(l_ref[...], w_ref[ATTN_WIDTH:, :], preferred_element_type=jnp.float32))
    o_ref[...] = o * _rms_scale(o) * gf_ref[...]


def _out_proj(x2, attn, proj, mixed_lru, g_attn, w_out, g_final):
    m, d = x2.shape
    z_blk = 3 * ATTN_WIDTH // ATTN_WIDTH
    return pl.pallas_call(
        _out_proj_kernel,
        grid=(m // TM_OUT,),
        in_specs=[
            pl.BlockSpec((TM_OUT, d), lambda i: (i, 0)),
            pl.BlockSpec((TM_OUT, ATTN_WIDTH), lambda i: (i, 0)),
            pl.BlockSpec((TM_OUT, ATTN_WIDTH), lambda i: (i, z_blk)),
            pl.BlockSpec((TM_OUT, LRU_WIDTH), lambda i: (i, 0)),
            pl.BlockSpec((1, ATTN_WIDTH), lambda i: (0, 0)),
            pl.BlockSpec((ATTN_WIDTH + LRU_WIDTH, d), lambda i: (0, 0), pipeline_mode=pl.Buffered(1)),
            pl.BlockSpec((1, d), lambda i: (0, 0)),
        ],
        out_specs=pl.BlockSpec((TM_OUT, d), lambda i: (i, 0)),
        out_shape=jax.ShapeDtypeStruct((m, d), jnp.float32),
        compiler_params=pltpu.CompilerParams(
            dimension_semantics=("arbitrary",), vmem_limit_bytes=VMEM_LIMIT_BYTES),
        name="out_proj",
    )(x2, attn, proj, mixed_lru, g_attn, w_out, g_final)


def kernel(x, norm_g, w_in, b_f, conv_w, conv_b, w_rg, b_rg, w_ig, b_ig, lru_lambda,
           attn_norm_g, lru_norm_g, w_out, final_norm_g):
    batch, seq_len, d = x.shape
    assert norm_g.shape[0] == 1, "single-layer trunk"
    assert seq_len % TM_IN == 0 and seq_len % TQ == 0 and seq_len % T_LRU == 0
    f_lo = 3 * ATTN_WIDTH
    f_hi = f_lo + ATTN_HEADS
    bf16 = jnp.bfloat16

    w0 = w_in[0]
    w_main = jnp.concatenate([w0[:, :f_lo], w0[:, f_hi:]], axis=1).astype(bf16)
    w_ft = jnp.pad(w0[:, f_lo:f_hi].T, ((0, HEAD_ROWS - ATTN_HEADS), (0, 0))).astype(bf16)
    bf_col = jnp.pad(b_f[0], (0, HEAD_ROWS - ATTN_HEADS)).reshape(HEAD_ROWS, 1)
    colscale = jnp.concatenate([jnp.full((1, ATTN_WIDTH), HEAD_DIM ** -0.5 * LOG2_E, jnp.float32),
                                jnp.ones((1, N_MAIN - ATTN_WIDTH), jnp.float32)], axis=1)
    w_gate = jnp.concatenate([w_rg[0], w_ig[0]], axis=-1).astype(bf16)
    row = lambda v: v.reshape(1, -1)

    x2 = x.reshape(batch * seq_len, d)
    proj, c_t = _in_proj(x2, row(norm_g[0]), w_main, colscale, w_ft, bf_col, seq_len)
    attn = _attention(proj, c_t, batch, seq_len)
    mixed_lru = _rglru(proj, conv_w[0], row(conv_b[0]), w_gate, row(b_rg[0]), row(b_ig[0]),
                       row(lru_lambda[0]), row(lru_norm_g[0]), batch, seq_len)
    out = _out_proj(x2, attn, proj, mixed_lru, row(attn_norm_g[0]), w_out[0].astype(bf16),
                    row(final_norm_g))
    return out.reshape(batch, seq_len, d)
```

```python
import functools

import jax
import jax.numpy as jnp
from jax import lax
from jax.experimental import pallas as pl
from jax.experimental.pallas import tpu as pltpu

ATTN_HEADS = 8
HEAD_DIM = 128
ATTN_WIDTH = ATTN_HEADS * HEAD_DIM
LRU_BLOCKS = 8
LRU_BLOCK_W = 128
LRU_WIDTH = LRU_BLOCKS * LRU_BLOCK_W
CONV_W = 4
LRU_C = 8.0
NORM_EPS = 1e-6
LOG2_E = 1.4426950408889634
N_MAIN = 3 * ATTN_WIDTH + ATTN_WIDTH + 2 * LRU_WIDTH

SUBLANES_F32 = 8
HEAD_ROWS = 16
VMEM_LIMIT_BYTES = 58 * 1024 * 1024

TM_IN = 512
TN_IN = 1024
CUM_CHUNK = 256
TQ = 1024
TK = 512
T_LRU = 256
TM_OUT = 512

_NEG_BIG = -1e30
SKIP_EXPONENT = 152.0


def _sigmoid(x):
    return 0.5 + 0.5 * jnp.tanh(0.5 * x)


def _log_sigmoid(x):
    return jnp.minimum(x, 0.0) - jnp.log1p(jnp.exp(-jnp.abs(x)))


def _softplus(x):
    return jnp.maximum(x, 0.0) + jnp.log1p(jnp.exp(-jnp.abs(x)))


def _rms_scale(x32):
    return lax.rsqrt(jnp.mean(x32 * x32, axis=-1, keepdims=True) + NORM_EPS)


def _in_proj_kernel(x_ref, g_ref, w_ref, cs_ref, wf_ref, bf_ref, proj_ref, c_ref, carry_ref,
                    *, steps_per_seq):
    i = pl.program_id(0)
    x = x_ref[...]
    h = (x * _rms_scale(x) * g_ref[...]).astype(jnp.bfloat16)

    for n in range(N_MAIN // TN_IN):
        cols = slice(n * TN_IN, (n + 1) * TN_IN)
        acc = jnp.dot(h, w_ref[:, cols], preferred_element_type=jnp.float32)
        proj_ref[:, cols] = (acc * cs_ref[:, cols]).astype(proj_ref.dtype)

    f = lax.dot_general(wf_ref[...], h, (((1,), (1,)), ((), ())),
                        preferred_element_type=jnp.float32)
    logf = _log_sigmoid(f + bf_ref[...]) * LOG2_E

    @pl.when(i % steps_per_seq == 0)
    def _():
        carry_ref[...] = jnp.zeros_like(carry_ref)

    r = lax.broadcasted_iota(jnp.int32, (CUM_CHUNK, CUM_CHUNK), 0)
    c = lax.broadcasted_iota(jnp.int32, (CUM_CHUNK, CUM_CHUNK), 1)
    tri = jnp.where(r <= c, 1.0, 0.0).astype(jnp.bfloat16)
    carry = carry_ref[:, 0:1]
    for n in range(TM_IN // CUM_CHUNK):
        cols = slice(n * CUM_CHUNK, (n + 1) * CUM_CHUNK)
        v = logf[:, cols]
        hi = v.astype(jnp.bfloat16)
        r1 = v - hi.astype(jnp.float32)
        mid = r1.astype(jnp.bfloat16)
        lo = (r1 - mid.astype(jnp.float32)).astype(jnp.bfloat16)
        cs = (jnp.dot(hi, tri, preferred_element_type=jnp.float32)
              + jnp.dot(mid, tri, preferred_element_type=jnp.float32)
              + jnp.dot(lo, tri, preferred_element_type=jnp.float32)) + carry
        c_ref[:, cols] = cs
        carry = cs[:, CUM_CHUNK - 1:CUM_CHUNK]
    carry_ref[...] = jnp.broadcast_to(carry, carry_ref.shape)


def _in_proj(x2, g, w_main, colscale, w_ft, b_f, seq_len):
    m, d = x2.shape
    kern = functools.partial(_in_proj_kernel, steps_per_seq=seq_len // TM_IN)
    return pl.pallas_call(
        kern,
        grid=(m // TM_IN,),
        in_specs=[
            pl.BlockSpec((TM_IN, d), lambda i: (i, 0)),
            pl.BlockSpec((1, d), lambda i: (0, 0)),
            pl.BlockSpec((d, N_MAIN), lambda i: (0, 0), pipeline_mode=pl.Buffered(1)),
            pl.BlockSpec((1, N_MAIN), lambda i: (0, 0)),
            pl.BlockSpec((HEAD_ROWS, d), lambda i: (0, 0)),
            pl.BlockSpec((HEAD_ROWS, 1), lambda i: (0, 0)),
        ],
        out_specs=[
            pl.BlockSpec((TM_IN, N_MAIN), lambda i: (i, 0)),
            pl.BlockSpec((HEAD_ROWS, TM_IN), lambda i: (0, i)),
        ],
        out_shape=[
            jax.ShapeDtypeStruct((m, N_MAIN), jnp.bfloat16),
            jax.ShapeDtypeStruct((HEAD_ROWS, m), jnp.float32),
        ],
        scratch_shapes=[pltpu.VMEM((HEAD_ROWS, 128), jnp.float32)],
        compiler_params=pltpu.CompilerParams(
            dimension_semantics=("arbitrary",), vmem_limit_bytes=VMEM_LIMIT_BYTES),
        name="in_proj",
    )(x2, g, w_main, colscale, w_ft, b_f)


def _attn_kernel(q_ref, k_ref, v_ref, c_ref, bnd_ref, o_ref, vx_ref, s_ref, m_ref, acc_ref, kn_ref):
    h = pl.program_id(1)
    qi = pl.program_id(2)
    lanes = HEAD_DIM
    sub = TQ // TK
    all_rows = (0, TQ)

    def row_norm2_max(x_bf16):
        x = x_bf16.astype(jnp.float32)
        return jnp.max(jnp.sum(x * x, axis=1, keepdims=True), axis=0, keepdims=True)

    @pl.when(qi == 0)
    def _():
        vx_ref[:, 0:lanes] = v_ref[...]
        vx_ref[:, lanes:] = jnp.ones((vx_ref.shape[0], lanes), vx_ref.dtype)
        kn_ref[...] = jnp.broadcast_to(row_norm2_max(k_ref[...]), kn_ref.shape)

    m_ref[...] = jnp.full_like(m_ref, _NEG_BIG)
    acc_ref[...] = jnp.zeros_like(acc_ref)

    lane_id = lax.broadcasted_iota(jnp.int32, (1, lanes), 1)
    c_end = bnd_ref[0:1, :]
    c_q = jnp.max(jnp.where(lane_id == qi, bnd_ref[1:2, :], -jnp.inf), axis=1, keepdims=True)
    slack = (c_end - c_q) - SKIP_EXPONENT
    qk2 = 4.0 * row_norm2_max(q_ref[...]) * kn_ref[0:1, 0:1]
    skippable = (slack > 0.0) & (slack * slack > qk2)
    n_off = qi * sub
    first_live = jnp.min(jnp.where(skippable | (lane_id >= n_off), n_off, lane_id))
    j0 = first_live // sub

    def scores(c, rows):
        r0, n = rows
        k0 = pl.multiple_of(c * TK, TK)
        s = lax.dot_general(q_ref[r0:r0 + n, :], k_ref[pl.ds(k0, TK), :], (((1,), (1,)), ((), ())),
                            preferred_element_type=jnp.float32)
        return s - c_ref[pl.ds(h, 1), pl.ds(k0, TK)]

    def step(c, slot, rows, masked, next_rows):
        r0, n = rows
        s = s_ref[slot, r0:r0 + n, :]
        if next_rows is not None:
            s_ref[1 - slot, next_rows[0]:next_rows[0] + next_rows[1], :] = scores(c + 1, next_rows)
        if masked:
            row = lax.broadcasted_iota(jnp.int32, (TK, TK), 0)
            col = lax.broadcasted_iota(jnp.int32, (TK, TK), 1)
            tri = jnp.where(col <= row, s[0:TK, :], _NEG_BIG)
            s = tri if n == TK else jnp.concatenate([tri, s[TK:, :]], axis=0)
        tiles = [s[:, i * lanes:(i + 1) * lanes] for i in range(TK // lanes)]
        m_cur = functools.reduce(jnp.maximum, tiles)
        m_prev = m_ref[r0:r0 + n, :]
        m_new = jnp.maximum(m_prev, jnp.max(m_cur, axis=1, keepdims=True))
        alpha = jnp.exp2(m_prev - m_new)
        p = jnp.concatenate([jnp.exp2(t - m_new) for t in tiles], axis=1).astype(jnp.bfloat16)
        k0 = pl.multiple_of(c * TK, TK)
        pv = jnp.dot(p, vx_ref[pl.ds(k0, TK), :], preferred_element_type=jnp.float32)
        acc_ref[r0:r0 + n, 0:lanes] = alpha * acc_ref[r0:r0 + n, 0:lanes] + pv[:, 0:lanes]
        acc_ref[r0:r0 + n, lanes:] = alpha * acc_ref[r0:r0 + n, lanes:] + pv[:, lanes:]
        m_ref[r0:r0 + n, :] = m_new

    s_ref[0] = scores(j0 * sub, all_rows)

    def body(j, carry):
        for si in range(sub):
            step(j * sub + si, si % 2, all_rows, False, all_rows)
        return carry

    lax.fori_loop(j0, qi, body, 0)
    for si in range(sub):
        nxt = ((si + 1) * TK, TQ - (si + 1) * TK) if si + 1 < sub else None
        step(qi * sub + si, si % 2, (si * TK, TQ - si * TK), True, nxt)
    o_ref[...] = (acc_ref[:, 0:lanes] / acc_ref[:, lanes:]).astype(o_ref.dtype)


def _attention(proj, c_t, batch, seq_len):
    assert TQ % TK == 0 and (TQ // TK) % 2 == 0
    m = proj.shape[0]
    nq = seq_len // TQ
    nch = seq_len // TK
    assert nch <= HEAD_DIM
    kv_off = ATTN_WIDTH // HEAD_DIM
    c3 = c_t[:ATTN_HEADS].reshape(ATTN_HEADS, batch, seq_len)
    pad = lambda a: jnp.pad(a, ((0, 0), (0, 0), (0, HEAD_DIM - a.shape[-1])))
    bounds = jnp.stack([pad(c3[:, :, TK - 1::TK]), pad(c3[:, :, ::TQ])], axis=2)
    return pl.pallas_call(
        _attn_kernel,
        grid=(batch, ATTN_HEADS, nq),
        in_specs=[
            pl.BlockSpec((TQ, HEAD_DIM), lambda b, h, qi: (b * nq + qi, h)),
            pl.BlockSpec((seq_len, HEAD_DIM), lambda b, h, qi: (b, kv_off + h)),
            pl.BlockSpec((seq_len, HEAD_DIM), lambda b, h, qi: (b, 2 * kv_off + h)),
            pl.BlockSpec((HEAD_ROWS, seq_len), lambda b, h, qi: (0, b)),
            pl.BlockSpec((None, None, 2, HEAD_DIM), lambda b, h, qi: (h, b, 0, 0)),
        ],
        out_specs=pl.BlockSpec((TQ, HEAD_DIM), lambda b, h, qi: (b * nq + qi, h)),
        out_shape=jax.ShapeDtypeStruct((m, ATTN_WIDTH), jnp.bfloat16),
        scratch_shapes=[
            pltpu.VMEM((seq_len, 2 * HEAD_DIM), jnp.bfloat16),
            pltpu.VMEM((2, TQ, TK), jnp.float32),
            pltpu.VMEM((TQ, HEAD_DIM), jnp.float32),
            pltpu.VMEM((TQ, 2 * HEAD_DIM), jnp.float32),
            pltpu.VMEM((SUBLANES_F32, HEAD_DIM), jnp.float32),
        ],
        compiler_params=pltpu.CompilerParams(
            dimension_semantics=("arbitrary", "arbitrary", "arbitrary"),
            vmem_limit_bytes=VMEM_LIMIT_BYTES),
        name="fox_attn",
    )(proj, proj, proj, c_t, bounds)


def _rglru_kernel(x_ref, z_ref, cw_ref, cb_ref, wg_ref, br_ref, bi_ref, lam_ref, g_ref,
                  o_ref, xs_ref, a_ref, b_ref, h_ref, hc_ref):
    t = pl.program_id(1)
    halo = SUBLANES_F32

    @pl.when(t == 0)
    def _():
        xs_ref[0:halo, :] = jnp.zeros((halo, LRU_WIDTH), jnp.float32)
        hc_ref[...] = jnp.zeros_like(hc_ref)

    xs_ref[halo:halo + T_LRU, :] = x_ref[...].astype(jnp.float32)
    u = cb_ref[...] + cw_ref[CONV_W - 1:CONV_W, :] * xs_ref[halo:halo + T_LRU, :]
    for k in range(CONV_W - 1):
        d = CONV_W - 1 - k
        u = u + cw_ref[k:k + 1, :] * xs_ref[halo - d:halo - d + T_LRU, :]
    xs_ref[0:halo, :] = xs_ref[T_LRU:T_LRU + halo, :]

    decay = LRU_C * _softplus(-lam_ref[...])
    decay2 = -LOG2_E * decay
    ub = u.astype(jnp.bfloat16)
    for n in range(LRU_BLOCKS):
        cols = slice(n * LRU_BLOCK_W, (n + 1) * LRU_BLOCK_W)
        gate = jnp.dot(ub[:, cols], wg_ref[n], preferred_element_type=jnp.float32)
        r = _sigmoid(gate[:, :LRU_BLOCK_W] + br_ref[:, cols])
        ig = _sigmoid(gate[:, LRU_BLOCK_W:] + bi_ref[:, cols])
        a = jnp.exp2(r * decay2[:, cols])
        a_ref[:, cols] = a
        one_minus_a2 = jnp.tanh(r * decay[:, cols]) * (1.0 + a * a)
        pos = one_minus_a2 > 0.0
        mult = jnp.where(pos, one_minus_a2 * lax.rsqrt(jnp.where(pos, one_minus_a2, 1.0)), 0.0)
        b_ref[:, cols] = mult * ig * u[:, cols]

    row = lax.broadcasted_iota(jnp.int32, (SUBLANES_F32, LRU_WIDTH), 0)

    def group(gidx, hprev):
        r0 = pl.multiple_of(gidx * SUBLANES_F32, SUBLANES_F32)
        av = a_ref[pl.ds(r0, SUBLANES_F32), :]
        bv = b_ref[pl.ds(r0, SUBLANES_F32), :]
        for d in (1, 2, 4):
            a_sh = jnp.where(row >= d, pltpu.roll(av, d, 0), 1.0)
            b_sh = jnp.where(row >= d, pltpu.roll(bv, d, 0), 0.0)
            bv = av * b_sh + bv
            av = av * a_sh
        hv = av * hprev + bv
        h_ref[pl.ds(r0, SUBLANES_F32), :] = hv
        return hv[SUBLANES_F32 - 1:SUBLANES_F32, :]

    hc_ref[0:1, :] = lax.fori_loop(0, T_LRU // SUBLANES_F32, group, hc_ref[0:1, :])

    hv = h_ref[...]
    z = z_ref[...].astype(jnp.float32)
    y = hv * _rms_scale(hv) * g_ref[...]
    o_ref[...] = (y * (z * _sigmoid(z))).astype(o_ref.dtype)


def _rglru(proj, conv_w, conv_b, w_gate, b_rg, b_ig, lam, g, batch, seq_len):
    m = proj.shape[0]
    nt = seq_len // T_LRU
    x_blk = (3 * ATTN_WIDTH + ATTN_WIDTH) // LRU_WIDTH
    vec = lambda: pl.BlockSpec((1, LRU_WIDTH), lambda b, t: (0, 0))
    return pl.pallas_call(
        _rglru_kernel,
        grid=(batch, nt),
        in_specs=[
            pl.BlockSpec((T_LRU, LRU_WIDTH), lambda b, t: (b * nt + t, x_blk)),
            pl.BlockSpec((T_LRU, LRU_WIDTH), lambda b, t: (b * nt + t, x_blk + 1)),
            pl.BlockSpec((CONV_W, LRU_WIDTH), lambda b, t: (0, 0)),
            vec(),
            pl.BlockSpec((LRU_BLOCKS, LRU_BLOCK_W, 2 * LRU_BLOCK_W), lambda b, t: (0, 0, 0)),
            vec(), vec(), vec(), vec(),
        ],
        out_specs=pl.BlockSpec((T_LRU, LRU_WIDTH), lambda b, t: (b * nt + t, 0)),
        out_shape=jax.ShapeDtypeStruct((m, LRU_WIDTH), jnp.bfloat16),
        scratch_shapes=[
            pltpu.VMEM((T_LRU + SUBLANES_F32, LRU_WIDTH), jnp.float32),
            pltpu.VMEM((T_LRU, LRU_WIDTH), jnp.float32),
            pltpu.VMEM((T_LRU, LRU_WIDTH), jnp.float32),
            pltpu.VMEM((T_LRU, LRU_WIDTH), jnp.float32),
            pltpu.VMEM((SUBLANES_F32, LRU_WIDTH), jnp.float32),
        ],
        compiler_params=pltpu.CompilerParams(
            dimension_semantics=("arbitrary", "arbitrary"), vmem_limit_bytes=VMEM_LIMIT_BYTES),
        name="rglru",
    )(proj, proj, conv_w, conv_b, w_gate, b_rg, b_ig, lam, g)


def _out_proj_kernel(x_ref, a_ref, z_ref, l_ref, ga_ref, w_ref, gf_ref, o_ref):
    a = a_ref[...].astype(jnp.float32)
    z = z_ref[...].astype(jnp.float32)
    ma = (a * _rms_scale(a) * ga_ref[...] * (z * _sigmoid(z))).astype(jnp.bfloat16)
    o = (x_ref[...]
         + jnp.dot(ma, w_ref[0:ATTN_WIDTH, :], preferred_element_type=jnp.float32)
         + jnp.dot(l_ref[...], w_ref[ATTN_WIDTH:, :], preferred_element_type=jnp.float32))
    o_ref[...] = o * _rms_scale(o) * gf_ref[...]


def _out_proj(x2, attn, proj, mixed_lru, g_attn, w_out, g_final):
    m, d = x2.shape
    z_blk = 3 * ATTN_WIDTH // ATTN_WIDTH
    return pl.pallas_call(
        _out_proj_kernel,
        grid=(m // TM_OUT,),
        in_specs=[
            pl.BlockSpec((TM_OUT, d), lambda i: (i, 0)),
            pl.BlockSpec((TM_OUT, ATTN_WIDTH), lambda i: (i, 0)),
            pl.BlockSpec((TM_OUT, ATTN_WIDTH), lambda i: (i, z_blk)),
            pl.BlockSpec((TM_OUT, LRU_WIDTH), lambda i: (i, 0)),
            pl.BlockSpec((1, ATTN_WIDTH), lambda i: (0, 0)),
            pl.BlockSpec((ATTN_WIDTH + LRU_WIDTH, d), lambda i: (0, 0), pipeline_mode=pl.Buffered(1)),
            pl.BlockSpec((1, d), lambda i: (0, 0)),
        ],
        out_specs=pl.BlockSpec((TM_OUT, d), lambda i: (i, 0)),
        out_shape=jax.ShapeDtypeStruct((m, d), jnp.float32),
        compiler_params=pltpu.CompilerParams(
            dimension_semantics=("arbitrary",), vmem_limit_bytes=VMEM_LIMIT_BYTES),
        name="out_proj",
    )(x2, attn, proj, mixed_lru, g_attn, w_out, g_final)


def kernel(x, norm_g, w_in, b_f, conv_w, conv_b, w_rg, b_rg, w_ig, b_ig, lru_lambda,
           attn_norm_g, lru_norm_g, w_out, final_norm_g):
    batch, seq_len, d = x.shape
    assert norm_g.shape[0] == 1, "single-layer trunk"
    assert seq_len % TM_IN == 0 and seq_len % TQ == 0 and seq_len % T_LRU == 0
    f_lo = 3 * ATTN_WIDTH
    f_hi = f_lo + ATTN_HEADS
    bf16 = jnp.bfloat16

    w0 = w_in[0]
    w_main = jnp.concatenate([w0[:, :f_lo], w0[:, f_hi:]], axis=1).astype(bf16)
    w_ft = jnp.pad(w0[:, f_lo:f_hi].T, ((0, HEAD_ROWS - ATTN_HEADS), (0, 0))).astype(bf16)
    bf_col = jnp.pad(b_f[0], (0, HEAD_ROWS - ATTN_HEADS)).reshape(HEAD_ROWS, 1)
    colscale = jnp.concatenate([jnp.full((1, ATTN_WIDTH), HEAD_DIM ** -0.5 * LOG2_E, jnp.float32),
                                jnp.ones((1, N_MAIN - ATTN_WIDTH), jnp.float32)], axis=1)
    w_gate = jnp.concatenate([w_rg[0], w_ig[0]], axis=-1).astype(bf16)
    row = lambda v: v.reshape(1, -1)

    x2 = x.reshape(batch * seq_len, d)
    proj, c_t = _in_proj(x2, row(norm_g[0]), w_main, colscale, w_ft, bf_col, seq_len)
    attn = _attention(proj, c_t, batch, seq_len)
    mixed_lru = _rglru(proj, conv_w[0], row(conv_b[0]), w_gate, row(b_rg[0]), row(b_ig[0]),
                       row(lru_lambda[0]), row(lru_norm_g[0]), batch, seq_len)
    out = _out_proj(x2, attn, proj, mixed_lru, row(attn_norm_g[0]), w_out[0].astype(bf16),
                    row(final_norm_g))
    return out.reshape(batch, seq_len, d)
```

```python
import functools

import jax
import jax.numpy as jnp
from jax import lax
from jax.experimental import pallas as pl
from jax.experimental.pallas import tpu as pltpu

ATTN_HEADS = 8
HEAD_DIM = 128
ATTN_WIDTH = ATTN_HEADS * HEAD_DIM
LRU_BLOCKS = 8
LRU_BLOCK_W = 128
LRU_WIDTH = LRU_BLOCKS * LRU_BLOCK_W
CONV_W = 4
LRU_C = 8.0
NORM_EPS = 1e-6
LOG2_E = 1.4426950408889634
N_MAIN = 3 * ATTN_WIDTH + ATTN_WIDTH + 2 * LRU_WIDTH

SUBLANES_F32 = 8
HEAD_ROWS = 16
VMEM_LIMIT_BYTES = 58 * 1024 * 1024

TM_IN = 512
TN_IN = 1024
CUM_CHUNK = 256
TQ = 1024
TK = 512
T_LRU = 256
TM_OUT = 512
TR_PACK = 256

_NEG_BIG = -1e30
SKIP_EXPONENT = 152.0


def _sigmoid(x):
    return 0.5 + 0.5 * jnp.tanh(0.5 * x)


def _log_sigmoid(x):
    return jnp.minimum(x, 0.0) - jnp.log1p(jnp.exp(-jnp.abs(x)))


def _softplus(x):
    return jnp.maximum(x, 0.0) + jnp.log1p(jnp.exp(-jnp.abs(x)))


def _rms_scale(x32):
    return lax.rsqrt(jnp.mean(x32 * x32, axis=-1, keepdims=True) + NORM_EPS)


def _in_proj_kernel(x_ref, g_ref, w_ref, cs_ref, wf_ref, bf_ref, proj_ref, c_ref, carry_ref,
                    *, steps_per_seq):
    i = pl.program_id(0)
    x = x_ref[...]
    h = (x * _rms_scale(x) * g_ref[...]).astype(jnp.bfloat16)

    for n in range(N_MAIN // TN_IN):
        cols = slice(n * TN_IN, (n + 1) * TN_IN)
        acc = jnp.dot(h, w_ref[:, cols], preferred_element_type=jnp.float32)
        proj_ref[:, cols] = (acc * cs_ref[:, cols]).astype(proj_ref.dtype)

    f = lax.dot_general(wf_ref[...], h, (((1,), (1,)), ((), ())),
                        preferred_element_type=jnp.float32)
    logf = _log_sigmoid(f + bf_ref[...]) * LOG2_E

    @pl.when(i % steps_per_seq == 0)
    def _():
        carry_ref[...] = jnp.zeros_like(carry_ref)

    r = lax.broadcasted_iota(jnp.int32, (CUM_CHUNK, CUM_CHUNK), 0)
    c = lax.broadcasted_iota(jnp.int32, (CUM_CHUNK, CUM_CHUNK), 1)
    tri = jnp.where(r <= c, 1.0, 0.0).astype(jnp.bfloat16)
    carry = carry_ref[:, 0:1]
    for n in range(TM_IN // CUM_CHUNK):
        cols = slice(n * CUM_CHUNK, (n + 1) * CUM_CHUNK)
        v = logf[:, cols]
        hi = v.astype(jnp.bfloat16)
        r1 = v - hi.astype(jnp.float32)
        mid = r1.astype(jnp.bfloat16)
        lo = (r1 - mid.astype(jnp.float32)).astype(jnp.bfloat16)
        cs = (jnp.dot(hi, tri, preferred_element_type=jnp.float32)
              + jnp.dot(mid, tri, preferred_element_type=jnp.float32)
              + jnp.dot(lo, tri, preferred_element_type=jnp.float32)) + carry
        c_ref[:, cols] = cs
        carry = cs[:, CUM_CHUNK - 1:CUM_CHUNK]
    carry_ref[...] = jnp.broadcast_to(carry, carry_ref.shape)


def _in_proj(x2, g, w_main, colscale, w_ft, b_f, seq_len):
    m, d = x2.shape
    kern = functools.partial(_in_proj_kernel, steps_per_seq=seq_len // TM_IN)
    return pl.pallas_call(
        kern,
        grid=(m // TM_IN,),
        in_specs=[
            pl.BlockSpec((TM_IN, d), lambda i: (i, 0)),
            pl.BlockSpec((1, d), lambda i: (0, 0)),
            pl.BlockSpec((d, N_MAIN), lambda i: (0, 0), pipeline_mode=pl.Buffered(1)),
            pl.BlockSpec((1, N_MAIN), lambda i: (0, 0)),
            pl.BlockSpec((HEAD_ROWS, d), lambda i: (0, 0)),
            pl.BlockSpec((HEAD_ROWS, 1), lambda i: (0, 0)),
        ],
        out_specs=[
            pl.BlockSpec((TM_IN, N_MAIN), lambda i: (i, 0)),
            pl.BlockSpec((HEAD_ROWS, TM_IN), lambda i: (0, i)),
        ],
        out_shape=[
            jax.ShapeDtypeStruct((m, N_MAIN), jnp.bfloat16),
            jax.ShapeDtypeStruct((HEAD_ROWS, m), jnp.float32),
        ],
        scratch_shapes=[pltpu.VMEM((HEAD_ROWS, 128), jnp.float32)],
        compiler_params=pltpu.CompilerParams(
            dimension_semantics=("arbitrary",), vmem_limit_bytes=VMEM_LIMIT_BYTES),
        name="in_proj",
    )(x2, g, w_main, colscale, w_ft, b_f)


def _attn_kernel(q_ref, k_ref, v_ref, c_ref, bnd_ref, o_ref, vx_ref, s_ref, m_ref, acc_ref, kn_ref):
    h = pl.program_id(1)
    qi = pl.program_id(2)
    lanes = HEAD_DIM
    sub = TQ // TK
    all_rows = (0, TQ)

    def row_norm2_max(x_bf16):
        x = x_bf16.astype(jnp.float32)
        return jnp.max(jnp.sum(x * x, axis=1, keepdims=True), axis=0, keepdims=True)

    @pl.when(qi == 0)
    def _():
        vx_ref[:, 0:lanes] = v_ref[...]
        vx_ref[:, lanes:] = jnp.ones((vx_ref.shape[0], lanes), vx_ref.dtype)
        kn_ref[...] = jnp.broadcast_to(row_norm2_max(k_ref[...]), kn_ref.shape)

    def scores(c, rows):
        r0, n = rows
        k0 = pl.multiple_of(c * TK, TK)
        s = lax.dot_general(q_ref[r0:r0 + n, :], k_ref[pl.ds(k0, TK), :], (((1,), (1,)), ((), ())),
                            preferred_element_type=jnp.float32)
        return s - c_ref[pl.ds(h, 1), pl.ds(k0, TK)]

    def step(c, slot, rows, masked, first, next_c, next_rows):
        r0, n = rows
        s = s_ref[slot, r0:r0 + n, :]
        s_ref[1 - slot, next_rows[0]:next_rows[0] + next_rows[1], :] = scores(next_c, next_rows)
        if masked:
            row = lax.broadcasted_iota(jnp.int32, (TK, TK), 0)
            col = lax.broadcasted_iota(jnp.int32, (TK, TK), 1)
            tri = jnp.where(col <= row, s[0:TK, :], _NEG_BIG)
            s = tri if n == TK else jnp.concatenate([tri, s[TK:, :]], axis=0)
        tiles = [s[:, i * lanes:(i + 1) * lanes] for i in range(TK // lanes)]
        m_cur = jnp.max(functools.reduce(jnp.maximum, tiles), axis=1, keepdims=True)
        k0 = pl.multiple_of(c * TK, TK)
        if first:
            m_new = jnp.broadcast_to(m_cur, (n, lanes))
        else:
            m_prev = m_ref[r0:r0 + n, :]
            m_new = jnp.maximum(m_prev, m_cur)
            alpha = jnp.exp2(m_prev - m_new)
        p = jnp.concatenate([jnp.exp2(t - m_new) for t in tiles], axis=1).astype(jnp.bfloat16)
        pv = jnp.dot(p, vx_ref[pl.ds(k0, TK), :], preferred_element_type=jnp.float32)
        if first:
            acc_ref[r0:r0 + n, :] = pv
        else:
            acc_ref[r0:r0 + n, 0:lanes] = alpha * acc_ref[r0:r0 + n, 0:lanes] + pv[:, 0:lanes]
            acc_ref[r0:r0 + n, lanes:] = alpha * acc_ref[r0:r0 + n, lanes:] + pv[:, lanes:]
        m_ref[r0:r0 + n, :] = m_new

    n_off = qi * sub
    s_ref[0] = scores(n_off, all_rows)

    lane_id = lax.broadcasted_iota(jnp.int32, (1, lanes), 1)
    c_end = bnd_ref[0:1, :]
    c_q = jnp.max(jnp.where(lane_id == qi, bnd_ref[1:2, :], -jnp.inf), axis=1, keepdims=True)
    slack = (c_end - c_q) - SKIP_EXPONENT
    qk2 = 4.0 * row_norm2_max(q_ref[...]) * kn_ref[0:1, 0:1]
    skippable = (slack > 0.0) & (slack * slack > qk2)
    first_live = jnp.min(jnp.where(skippable | (lane_id >= n_off), n_off, lane_id))
    pair0 = first_live // 2

    for si in range(sub):
        last = si + 1 == sub
        step(n_off + si, si % 2, (si * TK, TQ - si * TK), True, si == 0,
             jnp.maximum(n_off - 1, 0) if last else n_off + si + 1,
             all_rows if last else ((si + 1) * TK, TQ - (si + 1) * TK))

    def body(i, carry):
        hi = n_off - 1 - 2 * i
        step(hi, 0, all_rows, False, False, hi - 1, all_rows)
        step(hi - 1, 1, all_rows, False, False, jnp.maximum(hi - 2, 0), all_rows)
        return carry

    lax.fori_loop(0, n_off // 2 - pair0, body, 0)
    o_ref[...] = (acc_ref[:, 0:lanes] / acc_ref[:, lanes:]).astype(o_ref.dtype)


def _attention(proj, c_t, batch, seq_len):
    assert TQ % TK == 0 and (TQ // TK) % 2 == 0
    m = proj.shape[0]
    nq = seq_len // TQ
    nch = seq_len // TK
    assert nch <= HEAD_DIM
    kv_off = ATTN_WIDTH // HEAD_DIM
    c3 = c_t[:ATTN_HEADS].reshape(ATTN_HEADS, batch, seq_len)
    pad = lambda a: jnp.pad(a, ((0, 0), (0, 0), (0, HEAD_DIM - a.shape[-1])))
    bounds = jnp.stack([pad(c3[:, :, TK - 1::TK]), pad(c3[:, :, ::TQ])], axis=2)
    return pl.pallas_call(
        _attn_kernel,
        grid=(batch, ATTN_HEADS, nq),
        in_specs=[
            pl.BlockSpec((TQ, HEAD_DIM), lambda b, h, qi: (b * nq + qi, h)),
            pl.BlockSpec((seq_len, HEAD_DIM), lambda b, h, qi: (b, kv_off + h)),
            pl.BlockSpec((seq_len, HEAD_DIM), lambda b, h, qi: (b, 2 * kv_off + h)),
            pl.BlockSpec((HEAD_ROWS, seq_len), lambda b, h, qi: (0, b)),
            pl.BlockSpec((None, None, 2, HEAD_DIM), lambda b, h, qi: (h, b, 0, 0)),
        ],
        out_specs=pl.BlockSpec((TQ, HEAD_DIM), lambda b, h, qi: (b * nq + qi, h)),
        out_shape=jax.ShapeDtypeStruct((m, ATTN_WIDTH), jnp.bfloat16),
        scratch_shapes=[
            pltpu.VMEM((seq_len, 2 * HEAD_DIM), jnp.bfloat16),
            pltpu.VMEM((2, TQ, TK), jnp.float32),
            pltpu.VMEM((TQ, HEAD_DIM), jnp.float32),
            pltpu.VMEM((TQ, 2 * HEAD_DIM), jnp.float32),
            pltpu.VMEM((SUBLANES_F32, HEAD_DIM), jnp.float32),
        ],
        compiler_params=pltpu.CompilerParams(
            dimension_semantics=("arbitrary", "arbitrary", "arbitrary"),
            vmem_limit_bytes=VMEM_LIMIT_BYTES),
        name="fox_attn",
    )(proj, proj, proj, c_t, bounds)


def _rglru_kernel(x_ref, z_ref, cw_ref, cb_ref, wg_ref, br_ref, bi_ref, lam_ref, g_ref,
                  o_ref, xs_ref, a_ref, b_ref, h_ref, hc_ref):
    t = pl.program_id(1)
    halo = SUBLANES_F32

    @pl.when(t == 0)
    def _():
        xs_ref[0:halo, :] = jnp.zeros((halo, LRU_WIDTH), jnp.float32)
        hc_ref[...] = jnp.zeros_like(hc_ref)

    xs_ref[halo:halo + T_LRU, :] = x_ref[...].astype(jnp.float32)
    u = cb_ref[...] + cw_ref[CONV_W - 1:CONV_W, :] * xs_ref[halo:halo + T_LRU, :]
    for k in range(CONV_W - 1):
        d = CONV_W - 1 - k
        u = u + cw_ref[k:k + 1, :] * xs_ref[halo - d:halo - d + T_LRU, :]
    xs_ref[0:halo, :] = xs_ref[T_LRU:T_LRU + halo, :]

    decay = LRU_C * _softplus(-lam_ref[...])
    decay2 = -LOG2_E * decay
    ub = u.astype(jnp.bfloat16)
    for n in range(LRU_BLOCKS):
        cols = slice(n * LRU_BLOCK_W, (n + 1) * LRU_BLOCK_W)
        gate = jnp.dot(ub[:, cols], wg_ref[n], preferred_element_type=jnp.float32)
        r = _sigmoid(gate[:, :LRU_BLOCK_W] + br_ref[:, cols])
        ig = _sigmoid(gate[:, LRU_BLOCK_W:] + bi_ref[:, cols])
        a = jnp.exp2(r * decay2[:, cols])
        a_ref[:, cols] = a
        one_minus_a2 = jnp.tanh(r * decay[:, cols]) * (1.0 + a * a)
        pos = one_minus_a2 > 0.0
        mult = jnp.where(pos, one_minus_a2 * lax.rsqrt(jnp.where(pos, one_minus_a2, 1.0)), 0.0)
        b_ref[:, cols] = mult * ig * u[:, cols]

    row = lax.broadcasted_iota(jnp.int32, (SUBLANES_F32, LRU_WIDTH), 0)

    def group(gidx, hprev):
        r0 = pl.multiple_of(gidx * SUBLANES_F32, SUBLANES_F32)
        av = a_ref[pl.ds(r0, SUBLANES_F32), :]
        bv = b_ref[pl.ds(r0, SUBLANES_F32), :]
        for d in (1, 2, 4):
            a_sh = jnp.where(row >= d, pltpu.roll(av, d, 0), 1.0)
            b_sh = jnp.where(row >= d, pltpu.roll(bv, d, 0), 0.0)
            bv = av * b_sh + bv
            av = av * a_sh
        hv = av * hprev + bv
        h_ref[pl.ds(r0, SUBLANES_F32), :] = hv
        return hv[SUBLANES_F32 - 1:SUBLANES_F32, :]

    hc_ref[0:1, :] = lax.fori_loop(0, T_LRU // SUBLANES_F32, group, hc_ref[0:1, :])

    hv = h_ref[...]
    z = z_ref[...].astype(jnp.float32)
    y = hv * _rms_scale(hv) * g_ref[...]
    o_ref[...] = (y * (z * _sigmoid(z))).astype(o_ref.dtype)


def _rglru(proj, conv_w, conv_b, w_gate, b_rg, b_ig, lam, g, batch, seq_len):
    m = proj.shape[0]
    nt = seq_len // T_LRU
    x_blk = (3 * ATTN_WIDTH + ATTN_WIDTH) // LRU_WIDTH
    vec = lambda: pl.BlockSpec((1, LRU_WIDTH), lambda b, t: (0, 0))
    return pl.pallas_call(
        _rglru_kernel,
        grid=(batch, nt),
        in_specs=[
            pl.BlockSpec((T_LRU, LRU_WIDTH), lambda b, t: (b * nt + t, x_blk)),
            pl.BlockSpec((T_LRU, LRU_WIDTH), lambda b, t: (b * nt + t, x_blk + 1)),
            pl.BlockSpec((CONV_W, LRU_WIDTH), lambda b, t: (0, 0)),
            vec(),
            pl.BlockSpec((LRU_BLOCKS, LRU_BLOCK_W, 2 * LRU_BLOCK_W), lambda b, t: (0, 0, 0)),
            vec(), vec(), vec(), vec(),
        ],
        out_specs=pl.BlockSpec((T_LRU, LRU_WIDTH), lambda b, t: (b * nt + t, 0)),
        out_shape=jax.ShapeDtypeStruct((m, LRU_WIDTH), jnp.bfloat16),
        scratch_shapes=[
            pltpu.VMEM((T_LRU + SUBLANES_F32, LRU_WIDTH), jnp.float32),
            pltpu.VMEM((T_LRU, LRU_WIDTH), jnp.float32),
            pltpu.VMEM((T_LRU, LRU_WIDTH), jnp.float32),
            pltpu.VMEM((T_LRU, LRU_WIDTH), jnp.float32),
            pltpu.VMEM((SUBLANES_F32, LRU_WIDTH), jnp.float32),
        ],
        compiler_params=pltpu.CompilerParams(
            dimension_semantics=("arbitrary", "arbitrary"), vmem_limit_bytes=VMEM_LIMIT_BYTES),
        name="rglru",
    )(proj, proj, conv_w, conv_b, w_gate, b_rg, b_ig, lam, g)


def _out_proj_kernel(x_ref, a_ref, z_ref, l_ref, ga_ref, w_ref, gf_ref, o_ref):
    a = a_ref[...].astype(jnp.float32)
    z = z_ref[...].astype(jnp.float32)
    ma = (a * _rms_scale(a) * ga_ref[...] * (z * _sigmoid(z))).astype(jnp.bfloat16)
    o = (x_ref[...]
         + jnp.dot(ma, w_ref[0:ATTN_WIDTH, :], preferred_element_type=jnp.float32)
         + jnp.dot(l_ref[...], w_ref[ATTN_WIDTH:, :], preferred_element_type=jnp.float32))
    o_ref[...] = o * _rms_scale(o) * gf_ref[...]


def _out_proj(x2, attn, proj, mixed_lru, g_attn, w_out, g_final):
    m, d = x2.shape
    z_blk = 3 * ATTN_WIDTH // ATTN_WIDTH
    return pl.pallas_call(
        _out_proj_kernel,
        grid=(m // TM_OUT,),
        in_specs=[
            pl.BlockSpec((TM_OUT, d), lambda i: (i, 0)),
            pl.BlockSpec((TM_OUT, ATTN_WIDTH), lambda i: (i, 0)),
            pl.BlockSpec((TM_OUT, ATTN_WIDTH), lambda i: (i, z_blk)),
            pl.BlockSpec((TM_OUT, LRU_WIDTH), lambda i: (i, 0)),
            pl.BlockSpec((1, ATTN_WIDTH), lambda i: (0, 0)),
            pl.BlockSpec((ATTN_WIDTH + LRU_WIDTH, d), lambda i: (0, 0), pipeline_mode=pl.Buffered(1)),
            pl.BlockSpec((1, d), lambda i: (0, 0)),
        ],
        out_specs=pl.BlockSpec((TM_OUT, d), lambda i: (i, 0)),
        out_shape=jax.ShapeDtypeStruct((m, d), jnp.float32),
        compiler_params=pltpu.CompilerParams(
            dimension_semantics=("arbitrary",), vmem_limit_bytes=VMEM_LIMIT_BYTES),
        name="out_proj",
    )(x2, attn, proj, mixed_lru, g_attn, w_out, g_final)


def _pack_w_in_kernel(w_ref, o_ref):
    f_lo = 3 * ATTN_WIDTH
    f_hi = f_lo + ATTN_HEADS
    o_ref[:, 0:f_lo] = w_ref[:, 0:f_lo].astype(o_ref.dtype)
    o_ref[:, f_lo:] = w_ref[:, f_hi:].astype(o_ref.dtype)


def _pack_w_in(w0):
    d, n_in = w0.shape
    return pl.pallas_call(
        _pack_w_in_kernel,
        grid=(d // TR_PACK,),
        in_specs=[pl.BlockSpec((TR_PACK, n_in), lambda i: (i, 0))],
        out_specs=pl.BlockSpec((TR_PACK, N_MAIN), lambda i: (i, 0)),
        out_shape=jax.ShapeDtypeStruct((d, N_MAIN), jnp.bfloat16),
        compiler_params=pltpu.CompilerParams(
            dimension_semantics=("arbitrary",), vmem_limit_bytes=VMEM_LIMIT_BYTES),
        name="pack_w_in",
    )(w0)


def kernel(x, norm_g, w_in, b_f, conv_w, conv_b, w_rg, b_rg, w_ig, b_ig, lru_lambda,
           attn_norm_g, lru_norm_g, w_out, final_norm_g):
    batch, seq_len, d = x.shape
    assert norm_g.shape[0] == 1, "single-layer trunk"
    assert seq_len % TM_IN == 0 and seq_len % TQ == 0 and seq_len % T_LRU == 0
    f_lo = 3 * ATTN_WIDTH
    f_hi = f_lo + ATTN_HEADS
    bf16 = jnp.bfloat16

    w0 = w_in[0]
    w_main = _pack_w_in(w0)
    w_ft = jnp.pad(w0[:, f_lo:f_hi].T, ((0, HEAD_ROWS - ATTN_HEADS), (0, 0))).astype(bf16)
    bf_col = jnp.pad(b_f[0], (0, HEAD_ROWS - ATTN_HEADS)).reshape(HEAD_ROWS, 1)
    colscale = jnp.concatenate([jnp.full((1, ATTN_WIDTH), HEAD_DIM ** -0.5 * LOG2_E, jnp.float32),
                                jnp.ones((1, N_MAIN - ATTN_WIDTH), jnp.float32)], axis=1)
    w_gate = jnp.concatenate([w_rg[0], w_ig[0]], axis=-1).astype(bf16)
    row = lambda v: v.reshape(1, -1)

    x2 = x.reshape(batch * seq_len, d)
    proj, c_t = _in_proj(x2, row(norm_g[0]), w_main, colscale, w_ft, bf_col, seq_len)
    attn = _attention(proj, c_t, batch, seq_len)
    mixed_lru = _rglru(proj, conv_w[0], row(conv_b[0]), w_gate, row(b_rg[0]), row(b_ig[0]),
                       row(lru_lambda[0]), row(lru_norm_g[0]), batch, seq_len)
    out = _out_proj(x2, attn, proj, mixed_lru, row(attn_norm_g[0]), w_out[0].astype(bf16),
                    row(final_norm_g))
    return out.reshape(batch, seq_len, d)
```

```python
import functools

import jax
import jax.numpy as jnp
from jax import lax
from jax.experimental import pallas as pl
from jax.experimental.pallas import tpu as pltpu

ATTN_HEADS = 8
HEAD_DIM = 128
ATTN_WIDTH = ATTN_HEADS * HEAD_DIM
LRU_BLOCKS = 8
LRU_BLOCK_W = 128
LRU_WIDTH = LRU_BLOCKS * LRU_BLOCK_W
CONV_W = 4
LRU_C = 8.0
NORM_EPS = 1e-6
LOG2_E = 1.4426950408889634
Q_SCALE = HEAD_DIM ** -0.5 * LOG2_E
N_MAIN = 3 * ATTN_WIDTH + ATTN_WIDTH + 2 * LRU_WIDTH

SUBLANES_F32 = 8
HEAD_ROWS = 16
VMEM_LIMIT_BYTES = 58 * 1024 * 1024

TM_IN = 512
LRU_PHASES = 4
CUM_CHUNK = 256
TQ = 1024
TK = 512
TM_OUT = 512
TR_PACK = 256

_NEG_BIG = -1e30
SKIP_EXPONENT = 152.0


def _sigmoid(x):
    return 0.5 + 0.5 * jnp.tanh(0.5 * x)


def _log_sigmoid(x):
    return jnp.minimum(x, 0.0) - jnp.log1p(jnp.exp(-jnp.abs(x)))


def _softplus(x):
    return jnp.maximum(x, 0.0) + jnp.log1p(jnp.exp(-jnp.abs(x)))


def _rms_scale(x32):
    return lax.rsqrt(jnp.mean(x32 * x32, axis=-1, keepdims=True) + NORM_EPS)


def _in_proj_qk_kernel(x_ref, g_ref, w_ref, wf_ref, bf_ref, proj_ref, c_ref, carry_ref,
                       *, steps_per_seq):
    i = pl.program_id(0)
    x = x_ref[...]
    h = (x * _rms_scale(x) * g_ref[...]).astype(jnp.bfloat16)

    q = jnp.dot(h, w_ref[:, 0:ATTN_WIDTH], preferred_element_type=jnp.float32)
    proj_ref[:, 0:ATTN_WIDTH] = (q * Q_SCALE).astype(proj_ref.dtype)
    k = jnp.dot(h, w_ref[:, ATTN_WIDTH:], preferred_element_type=jnp.float32)
    proj_ref[:, ATTN_WIDTH:] = k.astype(proj_ref.dtype)

    f = lax.dot_general(wf_ref[...], h, (((1,), (1,)), ((), ())),
                        preferred_element_type=jnp.float32)
    logf = _log_sigmoid(f + bf_ref[...]) * LOG2_E

    @pl.when(i % steps_per_seq == 0)
    def _():
        carry_ref[...] = jnp.zeros_like(carry_ref)

    r = lax.broadcasted_iota(jnp.int32, (CUM_CHUNK, CUM_CHUNK), 0)
    c = lax.broadcasted_iota(jnp.int32, (CUM_CHUNK, CUM_CHUNK), 1)
    tri = jnp.where(r <= c, 1.0, 0.0).astype(jnp.bfloat16)
    carry = carry_ref[:, 0:1]
    for n in range(TM_IN // CUM_CHUNK):
        cols = slice(n * CUM_CHUNK, (n + 1) * CUM_CHUNK)
        v = logf[:, cols]
        hi = v.astype(jnp.bfloat16)
        r1 = v - hi.astype(jnp.float32)
        mid = r1.astype(jnp.bfloat16)
        lo = (r1 - mid.astype(jnp.float32)).astype(jnp.bfloat16)
        cs = (jnp.dot(hi, tri, preferred_element_type=jnp.float32)
              + jnp.dot(mid, tri, preferred_element_type=jnp.float32)
              + jnp.dot(lo, tri, preferred_element_type=jnp.float32)) + carry
        c_ref[:, cols] = cs
        carry = cs[:, CUM_CHUNK - 1:CUM_CHUNK]
    carry_ref[...] = jnp.broadcast_to(carry, carry_ref.shape)


def _in_proj_qk(x2, g, w_main, w_f, b_f, seq_len):
    m, d = x2.shape
    kern = functools.partial(_in_proj_qk_kernel, steps_per_seq=seq_len // TM_IN)
    return pl.pallas_call(
        kern,
        grid=(m // TM_IN,),
        in_specs=[
            pl.BlockSpec((TM_IN, d), lambda i: (i, 0)),
            pl.BlockSpec((1, d), lambda i: (0, 0)),
            pl.BlockSpec((d, 2 * ATTN_WIDTH), lambda i: (0, 0), pipeline_mode=pl.Buffered(1)),
            pl.BlockSpec((HEAD_ROWS, d), lambda i: (0, 0)),
            pl.BlockSpec((HEAD_ROWS, 1), lambda i: (0, 0)),
        ],
        out_specs=[
            pl.BlockSpec((TM_IN, 2 * ATTN_WIDTH), lambda i: (i, 0)),
            pl.BlockSpec((HEAD_ROWS, TM_IN), lambda i: (0, i)),
        ],
        out_shape=[
            jax.ShapeDtypeStruct((m, 2 * ATTN_WIDTH), jnp.bfloat16),
            jax.ShapeDtypeStruct((HEAD_ROWS, m), jnp.float32),
        ],
        scratch_shapes=[pltpu.VMEM((HEAD_ROWS, 128), jnp.float32)],
        compiler_params=pltpu.CompilerParams(
            dimension_semantics=("arbitrary",), vmem_limit_bytes=VMEM_LIMIT_BYTES),
        name="in_proj_qk",
    )(x2, g, w_main, w_f, b_f)


def _in_proj_lru_kernel(x_ref, g_ref, wvz_ref, wl_ref, cw_ref, cb_ref, wg_ref, br_ref, bi_ref,
                        lam_ref, gl_ref, proj_ref, mix_ref, h_ref, xs_ref, a_ref, b_ref, zl_ref,
                        hc_ref, *, steps_per_seq):
    i = pl.program_id(0)
    halo = SUBLANES_F32
    f32 = jnp.float32
    pw = LRU_WIDTH // LRU_PHASES
    x = x_ref[...]
    h_ref[...] = (x * _rms_scale(x) * g_ref[...]).astype(h_ref.dtype)

    @pl.when(i % steps_per_seq == 0)
    def _():
        xs_ref[0:halo, :] = jnp.zeros((halo, LRU_WIDTH), f32)
        hc_ref[...] = jnp.zeros_like(hc_ref)

    def project_x_lru(c0):
        xs_ref[halo:halo + TM_IN, c0:c0 + pw] = jnp.dot(
            h_ref[...], wl_ref[:, c0:c0 + pw], preferred_element_type=f32)

    def phase(j):
        c0 = j * pw
        cols = slice(c0, c0 + pw)
        if j + 1 < LRU_PHASES:
            project_x_lru(c0 + pw)
        vz = slice(2 * c0, 2 * (c0 + pw))
        proj_ref[:, vz] = jnp.dot(h_ref[...], wvz_ref[:, vz],
                                  preferred_element_type=f32).astype(proj_ref.dtype)
        zl_ref[:, cols] = jnp.dot(h_ref[...], wl_ref[:, LRU_WIDTH + c0:LRU_WIDTH + c0 + pw],
                                  preferred_element_type=f32)

        u = cb_ref[:, cols] + cw_ref[CONV_W - 1:CONV_W, cols] * xs_ref[halo:halo + TM_IN, cols]
        for k in range(CONV_W - 1):
            d = CONV_W - 1 - k
            u = u + cw_ref[k:k + 1, cols] * xs_ref[halo - d:halo - d + TM_IN, cols]
        xs_ref[0:halo, cols] = xs_ref[TM_IN:TM_IN + halo, cols]

        decay = LRU_C * _softplus(-lam_ref[:, cols])
        decay2 = -LOG2_E * decay
        ub = u.astype(jnp.bfloat16)
        for t in range(pw // LRU_BLOCK_W):
            lc = slice(t * LRU_BLOCK_W, (t + 1) * LRU_BLOCK_W)
            bc = slice(c0 + t * LRU_BLOCK_W, c0 + (t + 1) * LRU_BLOCK_W)
            gate = jnp.dot(ub[:, lc], wg_ref[c0 // LRU_BLOCK_W + t], preferred_element_type=f32)
            r = _sigmoid(gate[:, :LRU_BLOCK_W] + br_ref[:, bc])
            ig = _sigmoid(gate[:, LRU_BLOCK_W:] + bi_ref[:, bc])
            a = jnp.exp2(r * decay2[:, lc])
            a_ref[:, bc] = a
            one_minus_a2 = jnp.tanh(r * decay[:, lc]) * (1.0 + a * a)
            pos = one_minus_a2 > 0.0
            mult = jnp.where(pos, one_minus_a2 * lax.rsqrt(jnp.where(pos, one_minus_a2, 1.0)), 0.0)
            b_ref[:, bc] = mult * ig * u[:, lc]

        row = lax.broadcasted_iota(jnp.int32, (SUBLANES_F32, pw), 0)
        hprev = hc_ref[0:1, cols]
        for gidx in range(TM_IN // SUBLANES_F32):
            rows = slice(gidx * SUBLANES_F32, (gidx + 1) * SUBLANES_F32)
            av = a_ref[rows, cols]
            bv = b_ref[rows, cols]
            for d in (1, 2, 4):
                a_sh = jnp.where(row >= d, pltpu.roll(av, d, 0), 1.0)
                b_sh = jnp.where(row >= d, pltpu.roll(bv, d, 0), 0.0)
                bv = av * b_sh + bv
                av = av * a_sh
            hv = av * hprev + bv
            b_ref[rows, cols] = hv
            hprev = hv[SUBLANES_F32 - 1:SUBLANES_F32, :]
        hc_ref[0:1, cols] = hprev

    always = i >= 0
    project_x_lru(0)
    for j in range(LRU_PHASES):
        pl.when(always)(functools.partial(phase, j))

    hv = b_ref[...]
    z = zl_ref[...]
    y = hv * _rms_scale(hv) * gl_ref[...]
    mix_ref[...] = (y * (z * _sigmoid(z))).astype(mix_ref.dtype)


def _in_proj_lru(x2, g, w_main, conv_w, conv_b, w_gate, b_rg, b_ig, lam, g_lru, seq_len):
    m, d = x2.shape
    kern = functools.partial(_in_proj_lru_kernel, steps_per_seq=seq_len // TM_IN)
    vec = lambda: pl.BlockSpec((1, LRU_WIDTH), lambda i: (0, 0))
    wblk = lambda j: pl.BlockSpec((d, 2 * ATTN_WIDTH), lambda i: (0, j), pipeline_mode=pl.Buffered(1))
    return pl.pallas_call(
        kern,
        grid=(m // TM_IN,),
        in_specs=[
            pl.BlockSpec((TM_IN, d), lambda i: (i, 0)),
            pl.BlockSpec((1, d), lambda i: (0, 0)),
            wblk(1),
            wblk(2),
            pl.BlockSpec((CONV_W, LRU_WIDTH), lambda i: (0, 0)),
            vec(),
            pl.BlockSpec((LRU_BLOCKS, LRU_BLOCK_W, 2 * LRU_BLOCK_W), lambda i: (0, 0, 0)),
            vec(), vec(), vec(), vec(),
        ],
        out_specs=[
            pl.BlockSpec((TM_IN, 2 * ATTN_WIDTH), lambda i: (i, 0)),
            pl.BlockSpec((TM_IN, LRU_WIDTH), lambda i: (i, 0)),
        ],
        out_shape=[
            jax.ShapeDtypeStruct((m, 2 * ATTN_WIDTH), jnp.bfloat16),
            jax.ShapeDtypeStruct((m, LRU_WIDTH), jnp.bfloat16),
        ],
        scratch_shapes=[
            pltpu.VMEM((TM_IN, d), jnp.bfloat16),
            pltpu.VMEM((TM_IN + SUBLANES_F32, LRU_WIDTH), jnp.float32),
            pltpu.VMEM((TM_IN, LRU_WIDTH), jnp.float32),
            pltpu.VMEM((TM_IN, LRU_WIDTH), jnp.float32),
            pltpu.VMEM((TM_IN, LRU_WIDTH), jnp.float32),
            pltpu.VMEM((SUBLANES_F32, LRU_WIDTH), jnp.float32),
        ],
        compiler_params=pltpu.CompilerParams(
            dimension_semantics=("arbitrary",), vmem_limit_bytes=VMEM_LIMIT_BYTES),
        name="in_proj_lru",
    )(x2, g, w_main, w_main, conv_w, conv_b, w_gate, b_rg, b_ig, lam, g_lru)


def _attn_kernel(q_ref, k_ref, v_ref, c_ref, bnd_ref, o_ref, vx_ref, s_ref, m_ref, acc_ref, kn_ref):
    h = pl.program_id(1)
    qi = pl.program_id(2)
    lanes = HEAD_DIM
    sub = TQ // TK
    all_rows = (0, TQ)

    def row_norm2_max(x_bf16):
        x = x_bf16.astype(jnp.float32)
        return jnp.max(jnp.sum(x * x, axis=1, keepdims=True), axis=0, keepdims=True)

    @pl.when(qi == 0)
    def _():
        vx_ref[:, 0:lanes] = v_ref[...]
        vx_ref[:, lanes:] = jnp.ones((vx_ref.shape[0], lanes), vx_ref.dtype)
        kn_ref[...] = jnp.broadcast_to(row_norm2_max(k_ref[...]), kn_ref.shape)

    def scores(c, rows):
        r0, n = rows
        k0 = pl.multiple_of(c * TK, TK)
        s = lax.dot_general(q_ref[r0:r0 + n, :], k_ref[pl.ds(k0, TK), :], (((1,), (1,)), ((), ())),
                            preferred_element_type=jnp.float32)
        return s - c_ref[pl.ds(h, 1), pl.ds(k0, TK)]

    def step(c, slot, rows, masked, first, next_c, next_rows):
        r0, n = rows
        s = s_ref[slot, r0:r0 + n, :]
        s_ref[1 - slot, next_rows[0]:next_rows[0] + next_rows[1], :] = scores(next_c, next_rows)
        if masked:
            row = lax.broadcasted_iota(jnp.int32, (TK, TK), 0)
            col = lax.broadcasted_iota(jnp.int32, (TK, TK), 1)
            tri = jnp.where(col <= row, s[0:TK, :], _NEG_BIG)
            s = tri if n == TK else jnp.concatenate([tri, s[TK:, :]], axis=0)
        tiles = [s[:, i * lanes:(i + 1) * lanes] for i in range(TK // lanes)]
        m_cur = jnp.max(functools.reduce(jnp.maximum, tiles), axis=1, keepdims=True)
        k0 = pl.multiple_of(c * TK, TK)
        if first:
            m_new = jnp.broadcast_to(m_cur, (n, lanes))
        else:
            m_prev = m_ref[r0:r0 + n, :]
            m_new = jnp.maximum(m_prev, m_cur)
            alpha = jnp.exp2(m_prev - m_new)
        p = jnp.concatenate([jnp.exp2(t - m_new) for t in tiles], axis=1).astype(jnp.bfloat16)
        pv = jnp.dot(p, vx_ref[pl.ds(k0, TK), :], preferred_element_type=jnp.float32)
        if first:
            acc_ref[r0:r0 + n, :] = pv
        else:
            acc_ref[r0:r0 + n, 0:lanes] = alpha * acc_ref[r0:r0 + n, 0:lanes] + pv[:, 0:lanes]
            acc_ref[r0:r0 + n, lanes:] = alpha * acc_ref[r0:r0 + n, lanes:] + pv[:, lanes:]
        m_ref[r0:r0 + n, :] = m_new

    n_off = qi * sub
    s_ref[0] = scores(n_off, all_rows)

    lane_id = lax.broadcasted_iota(jnp.int32, (1, lanes), 1)
    c_end = bnd_ref[0:1, :]
    c_q = jnp.max(jnp.where(lane_id == qi, bnd_ref[1:2, :], -jnp.inf), axis=1, keepdims=True)
    slack = (c_end - c_q) - SKIP_EXPONENT
    qk2 = 4.0 * row_norm2_max(q_ref[...]) * kn_ref[0:1, 0:1]
    skippable = (slack > 0.0) & (slack * slack > qk2)
    first_live = jnp.min(jnp.where(skippable | (lane_id >= n_off), n_off, lane_id))
    pair0 = first_live // 2

    for si in range(sub):
        last = si + 1 == sub
        step(n_off + si, si % 2, (si * TK, TQ - si * TK), True, si == 0,
             jnp.maximum(n_off - 1, 0) if last else n_off + si + 1,
             all_rows if last else ((si + 1) * TK, TQ - (si + 1) * TK))

    def body(i, carry):
        hi = n_off - 1 - 2 * i
        step(hi, 0, all_rows, False, False, hi - 1, all_rows)
        step(hi - 1, 1, all_rows, False, False, jnp.maximum(hi - 2, 0), all_rows)
        return carry

    lax.fori_loop(0, n_off // 2 - pair0, body, 0)
    o_ref[...] = (acc_ref[:, 0:lanes] / acc_ref[:, lanes:]).astype(o_ref.dtype)


def _attention(proj_qk, proj_vz, c_t, batch, seq_len):
    assert TQ % TK == 0 and (TQ // TK) % 2 == 0
    m = proj_qk.shape[0]
    nq = seq_len // TQ
    nch = seq_len // TK
    assert nch <= HEAD_DIM
    k_off = ATTN_WIDTH // HEAD_DIM
    c3 = c_t[:ATTN_HEADS].reshape(ATTN_HEADS, batch, seq_len)
    pad = lambda a: jnp.pad(a, ((0, 0), (0, 0), (0, HEAD_DIM - a.shape[-1])))
    bounds = jnp.stack([pad(c3[:, :, TK - 1::TK]), pad(c3[:, :, ::TQ])], axis=2)
    return pl.pallas_call(
        _attn_kernel,
        grid=(batch, ATTN_HEADS, nq),
        in_specs=[
            pl.BlockSpec((TQ, HEAD_DIM), lambda b, h, qi: (b * nq + qi, h)),
            pl.BlockSpec((seq_len, HEAD_DIM), lambda b, h, qi: (b, k_off + h)),
            pl.BlockSpec((seq_len, HEAD_DIM), lambda b, h, qi: (b, h)),
            pl.BlockSpec((HEAD_ROWS, seq_len), lambda b, h, qi: (0, b)),
            pl.BlockSpec((None, None, 2, HEAD_DIM), lambda b, h, qi: (h, b, 0, 0)),
        ],
        out_specs=pl.BlockSpec((TQ, HEAD_DIM), lambda b, h, qi: (b * nq + qi, h)),
        out_shape=jax.ShapeDtypeStruct((m, ATTN_WIDTH), jnp.bfloat16),
        scratch_shapes=[
            pltpu.VMEM((seq_len, 2 * HEAD_DIM), jnp.bfloat16),
            pltpu.VMEM((2, TQ, TK), jnp.float32),
            pltpu.VMEM((TQ, HEAD_DIM), jnp.float32),
            pltpu.VMEM((TQ, 2 * HEAD_DIM), jnp.float32),
            pltpu.VMEM((SUBLANES_F32, HEAD_DIM), jnp.float32),
        ],
        compiler_params=pltpu.CompilerParams(
            dimension_semantics=("arbitrary", "arbitrary", "arbitrary"),
            vmem_limit_bytes=VMEM_LIMIT_BYTES),
        name="fox_attn",
    )(proj_qk, proj_qk, proj_vz, c_t, bounds)


def _out_proj_kernel(x_ref, a_ref, z_ref, l_ref, ga_ref, w_ref, gf_ref, o_ref):
    a = a_ref[...].astype(jnp.float32)
    z = z_ref[...].astype(jnp.float32)
    ma = (a * _rms_scale(a) * ga_ref[...] * (z * _sigmoid(z))).astype(jnp.bfloat16)
    o = (x_ref[...]
         + jnp.dot(ma, w_ref[0:ATTN_WIDTH, :], preferred_element_type=jnp.float32)
         + jnp.dot(l_ref[...], w_ref[ATTN_WIDTH:, :], preferred_element_type=jnp.float32))
    o_ref[...] = o * _rms_scale(o) * gf_ref[...]


def _out_proj(x2, attn, proj_vz, mixed_lru, g_attn, w_out, g_final):
    m, d = x2.shape
    z_blk = 1
    return pl.pallas_call(
        _out_proj_kernel,
        grid=(m // TM_OUT,),
        in_specs=[
            pl.BlockSpec((TM_OUT, d), lambda i: (i, 0)),
            pl.BlockSpec((TM_OUT, ATTN_WIDTH), lambda i: (i, 0)),
            pl.BlockSpec((TM_OUT, ATTN_WIDTH), lambda i: (i, z_blk)),
            pl.BlockSpec((TM_OUT, LRU_WIDTH), lambda i: (i, 0)),
            pl.BlockSpec((1, ATTN_WIDTH), lambda i: (0, 0)),
            pl.BlockSpec((ATTN_WIDTH + LRU_WIDTH, d), lambda i: (0, 0), pipeline_mode=pl.Buffered(1)),
            pl.BlockSpec((1, d), lambda i: (0, 0)),
        ],
        out_specs=pl.BlockSpec((TM_OUT, d), lambda i: (i, 0)),
        out_shape=jax.ShapeDtypeStruct((m, d), jnp.float32),
        compiler_params=pltpu.CompilerParams(
            dimension_semantics=("arbitrary",), vmem_limit_bytes=VMEM_LIMIT_BYTES),
        name="out_proj",
    )(x2, attn, proj_vz, mixed_lru, g_attn, w_out, g_final)


def _pack_w_in_kernel(wt_ref, o_ref):
    f_lo = 3 * ATTN_WIDTH
    f_hi = f_lo + ATTN_HEADS
    o_ref[:, 0:f_lo] = wt_ref[0:f_lo, :].T.astype(o_ref.dtype)
    o_ref[:, f_lo:] = wt_ref[f_hi:, :].T.astype(o_ref.dtype)


def _pack_w_in(w0t):
    n_in, d = w0t.shape
    return pl.pallas_call(
        _pack_w_in_kernel,
        grid=(d // TR_PACK,),
        in_specs=[pl.BlockSpec((n_in, TR_PACK), lambda i: (0, i))],
        out_specs=pl.BlockSpec((TR_PACK, N_MAIN), lambda i: (i, 0)),
        out_shape=jax.ShapeDtypeStruct((d, N_MAIN), jnp.bfloat16),
        compiler_params=pltpu.CompilerParams(
            dimension_semantics=("arbitrary",), vmem_limit_bytes=VMEM_LIMIT_BYTES),
        name="pack_w_in",
    )(w0t)


def kernel(x, norm_g, w_in, b_f, conv_w, conv_b, w_rg, b_rg, w_ig, b_ig, lru_lambda,
           attn_norm_g, lru_norm_g, w_out, final_norm_g):
    batch, seq_len, d = x.shape
    assert norm_g.shape[0] == 1, "single-layer trunk"
    assert seq_len % TM_IN == 0 and seq_len % TQ == 0 and seq_len % TM_OUT == 0
    f_lo = 3 * ATTN_WIDTH
    f_hi = f_lo + ATTN_HEADS
    bf16 = jnp.bfloat16

    w0t = w_in[0].T
    w_main = _pack_w_in(w0t)
    w_f = jnp.pad(w0t[f_lo:f_hi], ((0, HEAD_ROWS - ATTN_HEADS), (0, 0))).astype(bf16)
    bf_col = jnp.pad(b_f[0], (0, HEAD_ROWS - ATTN_HEADS)).reshape(HEAD_ROWS, 1)
    w_gate = jnp.concatenate([w_rg[0], w_ig[0]], axis=-1).astype(bf16)
    row = lambda v: v.reshape(1, -1)

    x2 = x.reshape(batch * seq_len, d)
    g_in = row(norm_g[0])
    proj_qk, c_t = _in_proj_qk(x2, g_in, w_main, w_f, bf_col, seq_len)
    proj_vz, mixed_lru = _in_proj_lru(x2, g_in, w_main, conv_w[0], row(conv_b[0]), w_gate,
                                      row(b_rg[0]), row(b_ig[0]), row(lru_lambda[0]),
                                      row(lru_norm_g[0]), seq_len)
    attn = _attention(proj_qk, proj_vz, c_t, batch, seq_len)
    out = _out_proj(x2, attn, proj_vz, mixed_lru, row(attn_norm_g[0]), w_out[0].astype(bf16),
                    row(final_norm_g))
    return out.reshape(batch, seq_len, d)
```

```python
import functools

import jax
import jax.numpy as jnp
from jax import lax
from jax.experimental import pallas as pl
from jax.experimental.pallas import tpu as pltpu

ATTN_HEADS = 8
HEAD_DIM = 128
ATTN_WIDTH = ATTN_HEADS * HEAD_DIM
LRU_BLOCKS = 8
LRU_BLOCK_W = 128
LRU_WIDTH = LRU_BLOCKS * LRU_BLOCK_W
CONV_W = 4
LRU_C = 8.0
NORM_EPS = 1e-6
LOG2_E = 1.4426950408889634
Q_SCALE = HEAD_DIM ** -0.5 * LOG2_E
N_MAIN = 3 * ATTN_WIDTH + ATTN_WIDTH + 2 * LRU_WIDTH

SUBLANES_F32 = 8
HEAD_ROWS = 16
VMEM_LIMIT_BYTES = 58 * 1024 * 1024

TM_IN = 512
LRU_PHASES = 4
CUM_CHUNK = 256
TQ = 1024
TK = 512
TM_OUT = 512
TR_PACK = 256

_NEG_BIG = -1e30
SKIP_EXPONENT = 152.0


def _sigmoid(x):
    return 0.5 + 0.5 * jnp.tanh(0.5 * x)


def _log_sigmoid(x):
    return jnp.minimum(x, 0.0) - jnp.log1p(jnp.exp(-jnp.abs(x)))


def _softplus(x):
    return jnp.maximum(x, 0.0) + jnp.log1p(jnp.exp(-jnp.abs(x)))


def _rms_scale(x32):
    return lax.rsqrt(jnp.mean(x32 * x32, axis=-1, keepdims=True) + NORM_EPS)


def _in_proj_qk_kernel(x_ref, g_ref, w_ref, wf_ref, bf_ref, proj_ref, c_ref, h_ref, carry_ref,
                       *, steps_per_seq):
    i = pl.program_id(0)
    x = x_ref[...]
    h = (x * _rms_scale(x) * g_ref[...]).astype(jnp.bfloat16)
    h_ref[...] = h

    q = jnp.dot(h, w_ref[:, 0:ATTN_WIDTH], preferred_element_type=jnp.float32)
    proj_ref[:, 0:ATTN_WIDTH] = (q * Q_SCALE).astype(proj_ref.dtype)
    k = jnp.dot(h, w_ref[:, ATTN_WIDTH:], preferred_element_type=jnp.float32)
    proj_ref[:, ATTN_WIDTH:] = k.astype(proj_ref.dtype)

    f = lax.dot_general(wf_ref[...], h, (((1,), (1,)), ((), ())),
                        preferred_element_type=jnp.float32)
    logf = _log_sigmoid(f + bf_ref[...]) * LOG2_E

    @pl.when(i % steps_per_seq == 0)
    def _():
        carry_ref[...] = jnp.zeros_like(carry_ref)

    r = lax.broadcasted_iota(jnp.int32, (CUM_CHUNK, CUM_CHUNK), 0)
    c = lax.broadcasted_iota(jnp.int32, (CUM_CHUNK, CUM_CHUNK), 1)
    tri = jnp.where(r <= c, 1.0, 0.0).astype(jnp.bfloat16)
    carry = carry_ref[:, 0:1]
    for n in range(TM_IN // CUM_CHUNK):
        cols = slice(n * CUM_CHUNK, (n + 1) * CUM_CHUNK)
        v = logf[:, cols]
        hi = v.astype(jnp.bfloat16)
        r1 = v - hi.astype(jnp.float32)
        mid = r1.astype(jnp.bfloat16)
        lo = (r1 - mid.astype(jnp.float32)).astype(jnp.bfloat16)
        cs = (jnp.dot(hi, tri, preferred_element_type=jnp.float32)
              + jnp.dot(mid, tri, preferred_element_type=jnp.float32)
              + jnp.dot(lo, tri, preferred_element_type=jnp.float32)) + carry
        c_ref[:, cols] = cs
        carry = cs[:, CUM_CHUNK - 1:CUM_CHUNK]
    carry_ref[...] = jnp.broadcast_to(carry, carry_ref.shape)


def _in_proj_qk(x2, g, w_main, w_f, b_f, seq_len):
    m, d = x2.shape
    kern = functools.partial(_in_proj_qk_kernel, steps_per_seq=seq_len // TM_IN)
    return pl.pallas_call(
        kern,
        grid=(m // TM_IN,),
        in_specs=[
            pl.BlockSpec((TM_IN, d), lambda i: (i, 0)),
            pl.BlockSpec((1, d), lambda i: (0, 0)),
            pl.BlockSpec((d, 2 * ATTN_WIDTH), lambda i: (0, 0), pipeline_mode=pl.Buffered(1)),
            pl.BlockSpec((HEAD_ROWS, d), lambda i: (0, 0)),
            pl.BlockSpec((HEAD_ROWS, 1), lambda i: (0, 0)),
        ],
        out_specs=[
            pl.BlockSpec((TM_IN, 2 * ATTN_WIDTH), lambda i: (i, 0)),
            pl.BlockSpec((HEAD_ROWS, TM_IN), lambda i: (0, i)),
            pl.BlockSpec((TM_IN, d), lambda i: (i, 0)),
        ],
        out_shape=[
            jax.ShapeDtypeStruct((m, 2 * ATTN_WIDTH), jnp.bfloat16),
            jax.ShapeDtypeStruct((HEAD_ROWS, m), jnp.float32),
            jax.ShapeDtypeStruct((m, d), jnp.bfloat16),
        ],
        scratch_shapes=[pltpu.VMEM((HEAD_ROWS, 128), jnp.float32)],
        compiler_params=pltpu.CompilerParams(
            dimension_semantics=("arbitrary",), vmem_limit_bytes=VMEM_LIMIT_BYTES),
        name="in_proj_qk",
    )(x2, g, w_main, w_f, b_f)


def _in_proj_lru_kernel(h_ref, wvz_ref, wl_ref, cw_ref, cb_ref, wg_ref, br_ref, bi_ref,
                        lam_ref, gl_ref, proj_ref, mix_ref, xs_ref, a_ref, b_ref, zl_ref, hc_ref,
                        *, steps_per_seq):
    i = pl.program_id(0)
    halo = SUBLANES_F32
    f32 = jnp.float32
    pw = LRU_WIDTH // LRU_PHASES

    @pl.when(i % steps_per_seq == 0)
    def _():
        xs_ref[0:halo, :] = jnp.zeros((halo, LRU_WIDTH), f32)
        hc_ref[...] = jnp.zeros_like(hc_ref)

    def project_x_lru(c0):
        xs_ref[halo:halo + TM_IN, c0:c0 + pw] = jnp.dot(
            h_ref[...], wl_ref[:, c0:c0 + pw], preferred_element_type=f32)

    def phase(j):
        c0 = j * pw
        cols = slice(c0, c0 + pw)
        if j + 1 < LRU_PHASES:
            project_x_lru(c0 + pw)
        vz = slice(2 * c0, 2 * (c0 + pw))
        proj_ref[:, vz] = jnp.dot(h_ref[...], wvz_ref[:, vz],
                                  preferred_element_type=f32).astype(proj_ref.dtype)
        zl_ref[:, cols] = jnp.dot(h_ref[...], wl_ref[:, LRU_WIDTH + c0:LRU_WIDTH + c0 + pw],
                                  preferred_element_type=f32)

        u = cb_ref[:, cols] + cw_ref[CONV_W - 1:CONV_W, cols] * xs_ref[halo:halo + TM_IN, cols]
        for k in range(CONV_W - 1):
            d = CONV_W - 1 - k
            u = u + cw_ref[k:k + 1, cols] * xs_ref[halo - d:halo - d + TM_IN, cols]
        xs_ref[0:halo, cols] = xs_ref[TM_IN:TM_IN + halo, cols]

        decay = LRU_C * _softplus(-lam_ref[:, cols])
        decay2 = -LOG2_E * decay
        ub = u.astype(jnp.bfloat16)
        for t in range(pw // LRU_BLOCK_W):
            lc = slice(t * LRU_BLOCK_W, (t + 1) * LRU_BLOCK_W)
            bc = slice(c0 + t * LRU_BLOCK_W, c0 + (t + 1) * LRU_BLOCK_W)
            gate = jnp.dot(ub[:, lc], wg_ref[c0 // LRU_BLOCK_W + t], preferred_element_type=f32)
            r = _sigmoid(gate[:, :LRU_BLOCK_W] + br_ref[:, bc])
            ig = _sigmoid(gate[:, LRU_BLOCK_W:] + bi_ref[:, bc])
            a = jnp.exp2(r * decay2[:, lc])
            a_ref[:, bc] = a
            one_minus_a2 = jnp.tanh(r * decay[:, lc]) * (1.0 + a * a)
            pos = one_minus_a2 > 0.0
            mult = jnp.where(pos, one_minus_a2 * lax.rsqrt(jnp.where(pos, one_minus_a2, 1.0)), 0.0)
            b_ref[:, bc] = mult * ig * u[:, lc]

        row = lax.broadcasted_iota(jnp.int32, (SUBLANES_F32, pw), 0)
        hprev = hc_ref[0:1, cols]
        for gidx in range(TM_IN // SUBLANES_F32):
            rows = slice(gidx * SUBLANES_F32, (gidx + 1) * SUBLANES_F32)
            av = a_ref[rows, cols]
            bv = b_ref[rows, cols]
            for d in (1, 2, 4):
                a_sh = jnp.where(row >= d, pltpu.roll(av, d, 0), 1.0)
                b_sh = jnp.where(row >= d, pltpu.roll(bv, d, 0), 0.0)
                bv = av * b_sh + bv
                av = av * a_sh
            hv = av * hprev + bv
            b_ref[rows, cols] = hv
            hprev = hv[SUBLANES_F32 - 1:SUBLANES_F32, :]
        hc_ref[0:1, cols] = hprev

    always = i >= 0
    project_x_lru(0)
    for j in range(LRU_PHASES):
        pl.when(always)(functools.partial(phase, j))

    hv = b_ref[...]
    z = zl_ref[...]
    y = hv * _rms_scale(hv) * gl_ref[...]
    mix_ref[...] = (y * (z * _sigmoid(z))).astype(mix_ref.dtype)


def _in_proj_lru(h, w_main, conv_w, conv_b, w_gate, b_rg, b_ig, lam, g_lru, seq_len):
    m, d = h.shape
    kern = functools.partial(_in_proj_lru_kernel, steps_per_seq=seq_len // TM_IN)
    vec = lambda: pl.BlockSpec((1, LRU_WIDTH), lambda i: (0, 0))
    wblk = lambda j: pl.BlockSpec((d, 2 * ATTN_WIDTH), lambda i: (0, j), pipeline_mode=pl.Buffered(1))
    return pl.pallas_call(
        kern,
        grid=(m // TM_IN,),
        in_specs=[
            pl.BlockSpec((TM_IN, d), lambda i: (i, 0)),
            wblk(1),
            wblk(2),
            pl.BlockSpec((CONV_W, LRU_WIDTH), lambda i: (0, 0)),
            vec(),
            pl.BlockSpec((LRU_BLOCKS, LRU_BLOCK_W, 2 * LRU_BLOCK_W), lambda i: (0, 0, 0)),
            vec(), vec(), vec(), vec(),
        ],
        out_specs=[
            pl.BlockSpec((TM_IN, 2 * ATTN_WIDTH), lambda i: (i, 0)),
            pl.BlockSpec((TM_IN, LRU_WIDTH), lambda i: (i, 0)),
        ],
        out_shape=[
            jax.ShapeDtypeStruct((m, 2 * ATTN_WIDTH), jnp.bfloat16),
            jax.ShapeDtypeStruct((m, LRU_WIDTH), jnp.bfloat16),
        ],
        scratch_shapes=[
            pltpu.VMEM((TM_IN + SUBLANES_F32, LRU_WIDTH), jnp.float32),
            pltpu.VMEM((TM_IN, LRU_WIDTH), jnp.float32),
            pltpu.VMEM((TM_IN, LRU_WIDTH), jnp.float32),
            pltpu.VMEM((TM_IN, LRU_WIDTH), jnp.float32),
            pltpu.VMEM((SUBLANES_F32, LRU_WIDTH), jnp.float32),
        ],
        compiler_params=pltpu.CompilerParams(
            dimension_semantics=("arbitrary",), vmem_limit_bytes=VMEM_LIMIT_BYTES),
        name="in_proj_lru",
    )(h, w_main, w_main, conv_w, conv_b, w_gate, b_rg, b_ig, lam, g_lru)


def _attn_kernel(q_ref, k_ref, v_ref, c_ref, bnd_ref, o_ref, vx_ref, s_ref, m_ref, acc_ref, kn_ref):
    h = pl.program_id(1)
    qi = pl.program_id(2)
    nq = pl.num_programs(2)
    lanes = HEAD_DIM
    sub = TQ // TK
    all_rows = (0, TQ)

    def row_norm2_max(x_bf16):
        x = x_bf16.astype(jnp.float32)
        return jnp.max(jnp.sum(x * x, axis=1, keepdims=True), axis=0, keepdims=True)

    def scores(tile, c, rows):
        r0, n = rows
        q0 = pl.multiple_of(tile * TQ, TQ)
        k0 = pl.multiple_of(c * TK, TK)
        s = lax.dot_general(q_ref[pl.ds(q0 + r0, n), :], k_ref[pl.ds(k0, TK), :],
                            (((1,), (1,)), ((), ())), preferred_element_type=jnp.float32)
        return s - c_ref[pl.ds(h, 1), pl.ds(k0, TK)]

    @pl.when(qi == 0)
    def _():
        vx_ref[:, 0:lanes] = v_ref[...]
        vx_ref[:, lanes:] = jnp.ones((vx_ref.shape[0], lanes), vx_ref.dtype)
        kn_ref[...] = jnp.broadcast_to(row_norm2_max(k_ref[...]), kn_ref.shape)
        s_ref[0] = scores(0, 0, all_rows)

    def step(c, slot, rows, masked, first, next_tile, next_c, next_rows):
        r0, n = rows
        s = s_ref[slot, r0:r0 + n, :]
        s_ref[1 - slot, next_rows[0]:next_rows[0] + next_rows[1], :] = scores(
            next_tile, next_c, next_rows)
        if masked:
            row = lax.broadcasted_iota(jnp.int32, (TK, TK), 0)
            col = lax.broadcasted_iota(jnp.int32, (TK, TK), 1)
            tri = jnp.where(col <= row, s[0:TK, :], _NEG_BIG)
            s = tri if n == TK else jnp.concatenate([tri, s[TK:, :]], axis=0)
        tiles = [s[:, i * lanes:(i + 1) * lanes] for i in range(TK // lanes)]
        m_cur = jnp.max(functools.reduce(jnp.maximum, tiles), axis=1, keepdims=True)
        k0 = pl.multiple_of(c * TK, TK)
        if first:
            m_new = jnp.broadcast_to(m_cur, (n, lanes))
        else:
            m_prev = m_ref[r0:r0 + n, :]
            m_new = jnp.maximum(m_prev, m_cur)
            alpha = jnp.exp2(m_prev - m_new)
        p = jnp.concatenate([jnp.exp2(t - m_new) for t in tiles], axis=1).astype(jnp.bfloat16)
        pv = jnp.dot(p, vx_ref[pl.ds(k0, TK), :], preferred_element_type=jnp.float32)
        if first:
            acc_ref[r0:r0 + n, :] = pv
        else:
            acc_ref[r0:r0 + n, 0:lanes] = alpha * acc_ref[r0:r0 + n, 0:lanes] + pv[:, 0:lanes]
            acc_ref[r0:r0 + n, lanes:] = alpha * acc_ref[r0:r0 + n, lanes:] + pv[:, lanes:]
        m_ref[r0:r0 + n, :] = m_new

    n_off = qi * sub

    lane_id = lax.broadcasted_iota(jnp.int32, (1, lanes), 1)
    c_end = bnd_ref[0:1, :]
    c_q = jnp.max(jnp.where(lane_id == qi, bnd_ref[1:2, :], -jnp.inf), axis=1, keepdims=True)
    slack = (c_end - c_q) - SKIP_EXPONENT
    q_tile = q_ref[pl.ds(pl.multiple_of(qi * TQ, TQ), TQ), :]
    qk2 = 4.0 * row_norm2_max(q_tile) * kn_ref[0:1, 0:1]
    skippable = (slack > 0.0) & (slack * slack > qk2)
    first_live = jnp.min(jnp.where(skippable | (lane_id >= n_off), n_off, lane_id))
    trips = n_off // 2 - first_live // 2

    nxt_tile = jnp.minimum(qi + 1, nq - 1)
    for si in range(sub):
        if si + 1 < sub:
            step(n_off + si, si % 2, (si * TK, TQ - si * TK), True, si == 0,
                 qi, n_off + si + 1, ((si + 1) * TK, TQ - (si + 1) * TK))
        else:
            more = trips > 0
            step(n_off + si, si % 2, (si * TK, TQ - si * TK), True, si == 0,
                 jnp.where(more, qi, nxt_tile), jnp.where(more, n_off - 1, nxt_tile * sub),
                 all_rows)

    def body(i, carry):
        hi = n_off - 1 - 2 * i
        more = i + 1 < trips
        step(hi, 0, all_rows, False, False, qi, hi - 1, all_rows)
        step(hi - 1, 1, all_rows, False, False,
             jnp.where(more, qi, nxt_tile), jnp.where(more, hi - 2, nxt_tile * sub), all_rows)
        return carry

    lax.fori_loop(0, trips, body, 0)
    o_ref[...] = (acc_ref[:, 0:lanes] / acc_ref[:, lanes:]).astype(o_ref.dtype)


def _attention(proj_qk, proj_vz, c_t, batch, seq_len):
    assert TQ % TK == 0 and (TQ // TK) % 2 == 0
    m = proj_qk.shape[0]
    nq = seq_len // TQ
    nch = seq_len // TK
    assert nch <= HEAD_DIM
    k_off = ATTN_WIDTH // HEAD_DIM
    c3 = c_t[:ATTN_HEADS].reshape(ATTN_HEADS, batch, seq_len)
    pad = lambda a: jnp.pad(a, ((0, 0), (0, 0), (0, HEAD_DIM - a.shape[-1])))
    bounds = jnp.stack([pad(c3[:, :, TK - 1::TK]), pad(c3[:, :, ::TQ])], axis=2)
    return pl.pallas_call(
        _attn_kernel,
        grid=(batch, ATTN_HEADS, nq),
        in_specs=[
            pl.BlockSpec((seq_len, HEAD_DIM), lambda b, h, qi: (b, h)),
            pl.BlockSpec((seq_len, HEAD_DIM), lambda b, h, qi: (b, k_off + h)),
            pl.BlockSpec((seq_len, HEAD_DIM), lambda b, h, qi: (b, h)),
            pl.BlockSpec((HEAD_ROWS, seq_len), lambda b, h, qi: (0, b)),
            pl.BlockSpec((None, None, 2, HEAD_DIM), lambda b, h, qi: (h, b, 0, 0)),
        ],
        out_specs=pl.BlockSpec((TQ, HEAD_DIM), lambda b, h, qi: (b * nq + qi, h)),
        out_shape=jax.ShapeDtypeStruct((m, ATTN_WIDTH), jnp.bfloat16),
        scratch_shapes=[
            pltpu.VMEM((seq_len, 2 * HEAD_DIM), jnp.bfloat16),
            pltpu.VMEM((2, TQ, TK), jnp.float32),
            pltpu.VMEM((TQ, HEAD_DIM), jnp.float32),
            pltpu.VMEM((TQ, 2 * HEAD_DIM), jnp.float32),
            pltpu.VMEM((SUBLANES_F32, HEAD_DIM), jnp.float32),
        ],
        compiler_params=pltpu.CompilerParams(
            dimension_semantics=("arbitrary", "arbitrary", "arbitrary"),
            vmem_limit_bytes=VMEM_LIMIT_BYTES),
        name="fox_attn",
    )(proj_qk, proj_qk, proj_vz, c_t, bounds)


def _out_proj_kernel(x_ref, a_ref, z_ref, l_ref, ga_ref, w_ref, gf_ref, o_ref):
    a = a_ref[...].astype(jnp.float32)
    z = z_ref[...].astype(jnp.float32)
    ma = (a * _rms_scale(a) * ga_ref[...] * (z * _sigmoid(z))).astype(jnp.bfloat16)
    o = (x_ref[...]
         + jnp.dot(ma, w_ref[0:ATTN_WIDTH, :], preferred_element_type=jnp.float32)
         + jnp.dot(l_ref[...], w_ref[ATTN_WIDTH:, :], preferred_element_type=jnp.float32))
    o_ref[...] = o * _rms_scale(o) * gf_ref[...]


def _out_proj(x2, attn, proj_vz, mixed_lru, g_attn, w_out, g_final):
    m, d = x2.shape
    z_blk = 1
    return pl.pallas_call(
        _out_proj_kernel,
        grid=(m // TM_OUT,),
        in_specs=[
            pl.BlockSpec((TM_OUT, d), lambda i: (i, 0)),
            pl.BlockSpec((TM_OUT, ATTN_WIDTH), lambda i: (i, 0)),
            pl.BlockSpec((TM_OUT, ATTN_WIDTH), lambda i: (i, z_blk)),
            pl.BlockSpec((TM_OUT, LRU_WIDTH), lambda i: (i, 0)),
            pl.BlockSpec((1, ATTN_WIDTH), lambda i: (0, 0)),
            pl.BlockSpec((ATTN_WIDTH + LRU_WIDTH, d), lambda i: (0, 0), pipeline_mode=pl.Buffered(1)),
            pl.BlockSpec((1, d), lambda i: (0, 0)),
        ],
        out_specs=pl.BlockSpec((TM_OUT, d), lambda i: (i, 0)),
        out_shape=jax.ShapeDtypeStruct((m, d), jnp.float32),
        compiler_params=pltpu.CompilerParams(
            dimension_semantics=("arbitrary",), vmem_limit_bytes=VMEM_LIMIT_BYTES),
        name="out_proj",
    )(x2, attn, proj_vz, mixed_lru, g_attn, w_out, g_final)


def _pack_w_in_kernel(wt_ref, o_ref):
    f_lo = 3 * ATTN_WIDTH
    f_hi = f_lo + ATTN_HEADS
    o_ref[:, 0:f_lo] = wt_ref[0:f_lo, :].T.astype(o_ref.dtype)
    o_ref[:, f_lo:] = wt_ref[f_hi:, :].T.astype(o_ref.dtype)


def _pack_w_in(w0t):
    n_in, d = w0t.shape
    return pl.pallas_call(
        _pack_w_in_kernel,
        grid=(d // TR_PACK,),
        in_specs=[pl.BlockSpec((n_in, TR_PACK), lambda i: (0, i))],
        out_specs=pl.BlockSpec((TR_PACK, N_MAIN), lambda i: (i, 0)),
        out_shape=jax.ShapeDtypeStruct((d, N_MAIN), jnp.bfloat16),
        compiler_params=pltpu.CompilerParams(
            dimension_semantics=("arbitrary",), vmem_limit_bytes=VMEM_LIMIT_BYTES),
        name="pack_w_in",
    )(w0t)


def kernel(x, norm_g, w_in, b_f, conv_w, conv_b, w_rg, b_rg, w_ig, b_ig, lru_lambda,
           attn_norm_g, lru_norm_g, w_out, final_norm_g):
    batch, seq_len, d = x.shape
    assert norm_g.shape[0] == 1, "single-layer trunk"
    assert seq_len % TM_IN == 0 and seq_len % TQ == 0 and seq_len % TM_OUT == 0
    f_lo = 3 * ATTN_WIDTH
    f_hi = f_lo + ATTN_HEADS
    bf16 = jnp.bfloat16

    w0t = w_in[0].T
    w_main = _pack_w_in(w0t)
    w_f = jnp.pad(w0t[f_lo:f_hi], ((0, HEAD_ROWS - ATTN_HEADS), (0, 0))).astype(bf16)
    bf_col = jnp.pad(b_f[0], (0, HEAD_ROWS - ATTN_HEADS)).reshape(HEAD_ROWS, 1)
    w_gate = jnp.concatenate([w_rg[0], w_ig[0]], axis=-1).astype(bf16)
    row = lambda v: v.reshape(1, -1)

    x2 = x.reshape(batch * seq_len, d)
    g_in = row(norm_g[0])
    proj_qk, c_t, h = _in_proj_qk(x2, g_in, w_main, w_f, bf_col, seq_len)
    proj_vz, mixed_lru = _in_proj_lru(h, w_main, conv_w[0], row(conv_b[0]), w_gate,
                                      row(b_rg[0]), row(b_ig[0]), row(lru_lambda[0]),
                                      row(lru_norm_g[0]), seq_len)
    attn = _attention(proj_qk, proj_vz, c_t, batch, seq_len)
    out = _out_proj(x2, attn, proj_vz, mixed_lru, row(attn_norm_g[0]), w_out[0].astype(bf16),
                    row(final_norm_g))
    return out.reshape(batch, seq_len, d)
```

```python
import functools

import jax
import jax.numpy as jnp
from jax import lax
from jax.experimental import pallas as pl
from jax.experimental.pallas import tpu as pltpu

ATTN_HEADS = 8
HEAD_DIM = 128
ATTN_WIDTH = ATTN_HEADS * HEAD_DIM
LRU_BLOCKS = 8
LRU_BLOCK_W = 128
LRU_WIDTH = LRU_BLOCKS * LRU_BLOCK_W
CONV_W = 4
LRU_C = 8.0
NORM_EPS = 1e-6
LOG2_E = 1.4426950408889634
Q_SCALE = HEAD_DIM ** -0.5 * LOG2_E
N_MAIN = 3 * ATTN_WIDTH + ATTN_WIDTH + 2 * LRU_WIDTH

SUBLANES_F32 = 8
HEAD_ROWS = 16
VMEM_LIMIT_BYTES = 58 * 1024 * 1024

TM_IN = 512
LRU_PHASES = 4
CUM_CHUNK = 256
TQ = 1024
TK = 512
TM_OUT = 512
TR_PACK = 256

_NEG_BIG = -1e30
SKIP_EXPONENT = 152.0


def _sigmoid(x):
    return 0.5 + 0.5 * jnp.tanh(0.5 * x)


def _log_sigmoid(x):
    return jnp.minimum(x, 0.0) - jnp.log1p(jnp.exp(-jnp.abs(x)))


def _softplus(x):
    return jnp.maximum(x, 0.0) + jnp.log1p(jnp.exp(-jnp.abs(x)))


def _rms_scale(x32):
    return lax.rsqrt(jnp.mean(x32 * x32, axis=-1, keepdims=True) + NORM_EPS)


def _head_norm2_max(xb):
    x = xb.astype(jnp.float32)
    sq = x * x
    rows = []
    for hd in range(ATTN_HEADS):
        rs = jnp.sum(sq[:, hd * HEAD_DIM:(hd + 1) * HEAD_DIM], axis=1, keepdims=True)
        rows.append(jnp.broadcast_to(jnp.max(rs, axis=0, keepdims=True), (1, HEAD_DIM)))
    return jnp.concatenate(rows, axis=0)


def _in_proj_qk_kernel(x_ref, g_ref, w_ref, wf_ref, bf_ref, proj_ref, c_ref, h_ref, nrm_ref,
                       carry_ref, *, steps_per_seq):
    i = pl.program_id(0)
    x = x_ref[...]
    h = (x * _rms_scale(x) * g_ref[...]).astype(jnp.bfloat16)
    h_ref[...] = h

    q = jnp.dot(h, w_ref[:, 0:ATTN_WIDTH], preferred_element_type=jnp.float32)
    qb = (q * Q_SCALE).astype(proj_ref.dtype)
    proj_ref[:, 0:ATTN_WIDTH] = qb
    k = jnp.dot(h, w_ref[:, ATTN_WIDTH:], preferred_element_type=jnp.float32)
    kb = k.astype(proj_ref.dtype)
    proj_ref[:, ATTN_WIDTH:] = kb
    nrm_ref[...] = jnp.concatenate([_head_norm2_max(qb), _head_norm2_max(kb)], axis=0)

    f = lax.dot_general(wf_ref[...], h, (((1,), (1,)), ((), ())),
                        preferred_element_type=jnp.float32)
    logf = _log_sigmoid(f + bf_ref[...]) * LOG2_E

    @pl.when(i % steps_per_seq == 0)
    def _():
        carry_ref[...] = jnp.zeros_like(carry_ref)

    r = lax.broadcasted_iota(jnp.int32, (CUM_CHUNK, CUM_CHUNK), 0)
    c = lax.broadcasted_iota(jnp.int32, (CUM_CHUNK, CUM_CHUNK), 1)
    tri = jnp.where(r <= c, 1.0, 0.0).astype(jnp.bfloat16)
    carry = carry_ref[:, 0:1]
    for n in range(TM_IN // CUM_CHUNK):
        cols = slice(n * CUM_CHUNK, (n + 1) * CUM_CHUNK)
        v = logf[:, cols]
        hi = v.astype(jnp.bfloat16)
        r1 = v - hi.astype(jnp.float32)
        mid = r1.astype(jnp.bfloat16)
        lo = (r1 - mid.astype(jnp.float32)).astype(jnp.bfloat16)
        cs = (jnp.dot(hi, tri, preferred_element_type=jnp.float32)
              + jnp.dot(mid, tri, preferred_element_type=jnp.float32)
              + jnp.dot(lo, tri, preferred_element_type=jnp.float32)) + carry
        c_ref[:, cols] = cs
        carry = cs[:, CUM_CHUNK - 1:CUM_CHUNK]
    carry_ref[...] = jnp.broadcast_to(carry, carry_ref.shape)


def _in_proj_qk(x2, g, w_main, w_f, b_f, seq_len):
    m, d = x2.shape
    kern = functools.partial(_in_proj_qk_kernel, steps_per_seq=seq_len // TM_IN)
    return pl.pallas_call(
        kern,
        grid=(m // TM_IN,),
        in_specs=[
            pl.BlockSpec((TM_IN, d), lambda i: (i, 0)),
            pl.BlockSpec((1, d), lambda i: (0, 0)),
            pl.BlockSpec((d, 2 * ATTN_WIDTH), lambda i: (0, 0), pipeline_mode=pl.Buffered(1)),
            pl.BlockSpec((HEAD_ROWS, d), lambda i: (0, 0)),
            pl.BlockSpec((HEAD_ROWS, 1), lambda i: (0, 0)),
        ],
        out_specs=[
            pl.BlockSpec((TM_IN, 2 * ATTN_WIDTH), lambda i: (i, 0)),
            pl.BlockSpec((HEAD_ROWS, TM_IN), lambda i: (0, i)),
            pl.BlockSpec((TM_IN, d), lambda i: (i, 0)),
            pl.BlockSpec((None, 2 * ATTN_HEADS, HEAD_DIM), lambda i: (i, 0, 0)),
        ],
        out_shape=[
            jax.ShapeDtypeStruct((m, 2 * ATTN_WIDTH), jnp.bfloat16),
            jax.ShapeDtypeStruct((HEAD_ROWS, m), jnp.float32),
            jax.ShapeDtypeStruct((m, d), jnp.bfloat16),
            jax.ShapeDtypeStruct((m // TM_IN, 2 * ATTN_HEADS, HEAD_DIM), jnp.float32),
        ],
        scratch_shapes=[pltpu.VMEM((HEAD_ROWS, 128), jnp.float32)],
        compiler_params=pltpu.CompilerParams(
            dimension_semantics=("arbitrary",), vmem_limit_bytes=VMEM_LIMIT_BYTES),
        name="in_proj_qk",
    )(x2, g, w_main, w_f, b_f)


def _in_proj_lru_kernel(h_ref, wvz_ref, wl_ref, cw_ref, cb_ref, wg_ref, br_ref, bi_ref,
                        lam_ref, gl_ref, proj_ref, mix_ref, xs_ref, a_ref, b_ref, zl_ref, hc_ref,
                        *, steps_per_seq):
    i = pl.program_id(0)
    halo = SUBLANES_F32
    f32 = jnp.float32
    pw = LRU_WIDTH // LRU_PHASES

    @pl.when(i % steps_per_seq == 0)
    def _():
        xs_ref[0:halo, :] = jnp.zeros((halo, LRU_WIDTH), f32)
        hc_ref[...] = jnp.zeros_like(hc_ref)

    def project_x_lru(c0):
        xs_ref[halo:halo + TM_IN, c0:c0 + pw] = jnp.dot(
            h_ref[...], wl_ref[:, c0:c0 + pw], preferred_element_type=f32)

    def phase(j):
        c0 = j * pw
        cols = slice(c0, c0 + pw)
        if j + 1 < LRU_PHASES:
            project_x_lru(c0 + pw)
        vz = slice(2 * c0, 2 * (c0 + pw))
        proj_ref[:, vz] = jnp.dot(h_ref[...], wvz_ref[:, vz],
                                  preferred_element_type=f32).astype(proj_ref.dtype)
        zl_ref[:, cols] = jnp.dot(h_ref[...], wl_ref[:, LRU_WIDTH + c0:LRU_WIDTH + c0 + pw],
                                  preferred_element_type=f32)

        u = cb_ref[:, cols] + cw_ref[CONV_W - 1:CONV_W, cols] * xs_ref[halo:halo + TM_IN, cols]
        for k in range(CONV_W - 1):
            d = CONV_W - 1 - k
            u = u + cw_ref[k:k + 1, cols] * xs_ref[halo - d:halo - d + TM_IN, cols]
        xs_ref[0:halo, cols] = xs_ref[TM_IN:TM_IN + halo, cols]

        decay = LRU_C * _softplus(-lam_ref[:, cols])
        decay2 = -LOG2_E * decay
        ub = u.astype(jnp.bfloat16)
        for t in range(pw // LRU_BLOCK_W):
            lc = slice(t * LRU_BLOCK_W, (t + 1) * LRU_BLOCK_W)
            bc = slice(c0 + t * LRU_BLOCK_W, c0 + (t + 1) * LRU_BLOCK_W)
            gate = jnp.dot(ub[:, lc], wg_ref[c0 // LRU_BLOCK_W + t], preferred_element_type=f32)
            r = _sigmoid(gate[:, :LRU_BLOCK_W] + br_ref[:, bc])
            ig = _sigmoid(gate[:, LRU_BLOCK_W:] + bi_ref[:, bc])
            a = jnp.exp2(r * decay2[:, lc])
            a_ref[:, bc] = a
            one_minus_a2 = jnp.tanh(r * decay[:, lc]) * (1.0 + a * a)
            pos = one_minus_a2 > 0.0
            mult = jnp.where(pos, one_minus_a2 * lax.rsqrt(jnp.where(pos, one_minus_a2, 1.0)), 0.0)
            b_ref[:, bc] = mult * ig * u[:, lc]

        row = lax.broadcasted_iota(jnp.int32, (SUBLANES_F32, pw), 0)
        hprev = hc_ref[0:1, cols]
        for gidx in range(TM_IN // SUBLANES_F32):
            rows = slice(gidx * SUBLANES_F32, (gidx + 1) * SUBLANES_F32)
            av = a_ref[rows, cols]
            bv = b_ref[rows, cols]
            for d in (1, 2, 4):
                a_sh = jnp.where(row >= d, pltpu.roll(av, d, 0), 1.0)
                b_sh = jnp.where(row >= d, pltpu.roll(bv, d, 0), 0.0)
                bv = av * b_sh + bv
                av = av * a_sh
            hv = av * hprev + bv
            b_ref[rows, cols] = hv
            hprev = hv[SUBLANES_F32 - 1:SUBLANES_F32, :]
        hc_ref[0:1, cols] = hprev

    always = i >= 0
    project_x_lru(0)
    for j in range(LRU_PHASES):
        pl.when(always)(functools.partial(phase, j))

    hv = b_ref[...]
    z = zl_ref[...]
    y = hv * _rms_scale(hv) * gl_ref[...]
    mix_ref[...] = (y * (z * _sigmoid(z))).astype(mix_ref.dtype)


def _in_proj_lru(h, w_main, conv_w, conv_b, w_gate, b_rg, b_ig, lam, g_lru, seq_len):
    m, d = h.shape
    kern = functools.partial(_in_proj_lru_kernel, steps_per_seq=seq_len // TM_IN)
    vec = lambda: pl.BlockSpec((1, LRU_WIDTH), lambda i: (0, 0))
    wblk = lambda j: pl.BlockSpec((d, 2 * ATTN_WIDTH), lambda i: (0, j), pipeline_mode=pl.Buffered(1))
    return pl.pallas_call(
        kern,
        grid=(m // TM_IN,),
        in_specs=[
            pl.BlockSpec((TM_IN, d), lambda i: (i, 0)),
            wblk(1),
            wblk(2),
            pl.BlockSpec((CONV_W, LRU_WIDTH), lambda i: (0, 0)),
            vec(),
            pl.BlockSpec((LRU_BLOCKS, LRU_BLOCK_W, 2 * LRU_BLOCK_W), lambda i: (0, 0, 0)),
            vec(), vec(), vec(), vec(),
        ],
        out_specs=[
            pl.BlockSpec((TM_IN, 2 * ATTN_WIDTH), lambda i: (i, 0)),
            pl.BlockSpec((TM_IN, LRU_WIDTH), lambda i: (i, 0)),
        ],
        out_shape=[
            jax.ShapeDtypeStruct((m, 2 * ATTN_WIDTH), jnp.bfloat16),
            jax.ShapeDtypeStruct((m, LRU_WIDTH), jnp.bfloat16),
        ],
        scratch_shapes=[
            pltpu.VMEM((TM_IN + SUBLANES_F32, LRU_WIDTH), jnp.float32),
            pltpu.VMEM((TM_IN, LRU_WIDTH), jnp.float32),
            pltpu.VMEM((TM_IN, LRU_WIDTH), jnp.float32),
            pltpu.VMEM((TM_IN, LRU_WIDTH), jnp.float32),
            pltpu.VMEM((SUBLANES_F32, LRU_WIDTH), jnp.float32),
        ],
        compiler_params=pltpu.CompilerParams(
            dimension_semantics=("arbitrary",), vmem_limit_bytes=VMEM_LIMIT_BYTES),
        name="in_proj_lru",
    )(h, w_main, w_main, conv_w, conv_b, w_gate, b_rg, b_ig, lam, g_lru)


def _attn_kernel(q_ref, k_ref, v_ref, c_ref, bnd_ref, o_ref, vx_ref, s_ref, m_ref, acc_ref):
    h = pl.program_id(1)
    qi = pl.program_id(2)
    nq = pl.num_programs(2)
    lanes = HEAD_DIM
    sub = TQ // TK
    all_rows = (0, TQ)

    def scores(tile, c, rows):
        r0, n = rows
        q0 = pl.multiple_of(tile * TQ, TQ)
        k0 = pl.multiple_of(c * TK, TK)
        s = lax.dot_general(q_ref[pl.ds(q0 + r0, n), :], k_ref[pl.ds(k0, TK), :],
                            (((1,), (1,)), ((), ())), preferred_element_type=jnp.float32)
        return s - c_ref[pl.ds(h, 1), pl.ds(k0, TK)]

    @pl.when(qi == 0)
    def _():
        vx_ref[:, 0:lanes] = v_ref[...]
        vx_ref[:, lanes:] = jnp.ones((vx_ref.shape[0], lanes), vx_ref.dtype)
        s_ref[0] = scores(0, 0, all_rows)

    def step(c, slot, rows, masked, first, next_tile, next_c, next_rows):
        r0, n = rows
        s = s_ref[slot, r0:r0 + n, :]
        s_ref[1 - slot, next_rows[0]:next_rows[0] + next_rows[1], :] = scores(
            next_tile, next_c, next_rows)
        if masked:
            row = lax.broadcasted_iota(jnp.int32, (TK, TK), 0)
            col = lax.broadcasted_iota(jnp.int32, (TK, TK), 1)
            tri = jnp.where(col <= row, s[0:TK, :], _NEG_BIG)
            s = tri if n == TK else jnp.concatenate([tri, s[TK:, :]], axis=0)
        tiles = [s[:, i * lanes:(i + 1) * lanes] for i in range(TK // lanes)]
        m_cur = jnp.max(functools.reduce(jnp.maximum, tiles), axis=1, keepdims=True)
        k0 = pl.multiple_of(c * TK, TK)
        if first:
            m_new = jnp.broadcast_to(m_cur, (n, lanes))
        else:
            m_prev = m_ref[r0:r0 + n, :]
            m_new = jnp.maximum(m_prev, m_cur)
            alpha = jnp.exp2(m_prev - m_new)
        p = jnp.concatenate([jnp.exp2(t - m_new) for t in tiles], axis=1).astype(jnp.bfloat16)
        pv = jnp.dot(p, vx_ref[pl.ds(k0, TK), :], preferred_element_type=jnp.float32)
        if first:
            acc_ref[r0:r0 + n, :] = pv
        else:
            acc_ref[r0:r0 + n, 0:lanes] = alpha * acc_ref[r0:r0 + n, 0:lanes] + pv[:, 0:lanes]
            acc_ref[r0:r0 + n, lanes:] = alpha * acc_ref[r0:r0 + n, lanes:] + pv[:, lanes:]
        m_ref[r0:r0 + n, :] = m_new

    n_off = qi * sub

    lane_id = lax.broadcasted_iota(jnp.int32, (1, lanes), 1)
    c_end = bnd_ref[0:1, :]
    pick = lambda r: jnp.max(jnp.where(lane_id == qi, bnd_ref[r:r + 1, :], -jnp.inf),
                             axis=1, keepdims=True)
    c_q = pick(1)
    slack = (c_end - c_q) - SKIP_EXPONENT
    qk2 = 4.0 * pick(2) * bnd_ref[3:4, 0:1]
    skippable = (slack > 0.0) & (slack * slack > qk2)
    first_live = jnp.min(jnp.where(skippable | (lane_id >= n_off), n_off, lane_id))
    trips = n_off // 2 - first_live // 2

    nxt_tile = jnp.minimum(qi + 1, nq - 1)
    for si in range(sub):
        if si + 1 < sub:
            step(n_off + si, si % 2, (si * TK, TQ - si * TK), True, si == 0,
                 qi, n_off + si + 1, ((si + 1) * TK, TQ - (si + 1) * TK))
        else:
            more = trips > 0
            step(n_off + si, si % 2, (si * TK, TQ - si * TK), True, si == 0,
                 jnp.where(more, qi, nxt_tile), jnp.where(more, n_off - 1, nxt_tile * sub),
                 all_rows)

    def body(i, carry):
        hi = n_off - 1 - 2 * i
        more = i + 1 < trips
        step(hi, 0, all_rows, False, False, qi, hi - 1, all_rows)
        step(hi - 1, 1, all_rows, False, False,
             jnp.where(more, qi, nxt_tile), jnp.where(more, hi - 2, nxt_tile * sub), all_rows)
        return carry

    lax.fori_loop(0, trips, body, 0)
    o_ref[...] = (acc_ref[:, 0:lanes] / acc_ref[:, lanes:]).astype(o_ref.dtype)


def _attention(proj_qk, proj_vz, c_t, norms, batch, seq_len):
    assert TQ % TK == 0 and (TQ // TK) % 2 == 0
    assert TQ % TM_IN == 0
    m = proj_qk.shape[0]
    nq = seq_len // TQ
    nch = seq_len // TK
    assert nch <= HEAD_DIM
    k_off = ATTN_WIDTH // HEAD_DIM
    c3 = c_t[:ATTN_HEADS].reshape(ATTN_HEADS, batch, seq_len)
    n2 = norms[:, :, 0].reshape(batch, nq, TQ // TM_IN, 2 * ATTN_HEADS)
    qn = n2[..., :ATTN_HEADS].max(axis=2).transpose(2, 0, 1)
    kn = n2[..., ATTN_HEADS:].max(axis=(1, 2)).T[:, :, None]
    pad = lambda a: jnp.pad(a, ((0, 0), (0, 0), (0, HEAD_DIM - a.shape[-1])))
    bounds = jnp.stack([pad(c3[:, :, TK - 1::TK]), pad(c3[:, :, ::TQ]), pad(qn),
                        jnp.broadcast_to(kn, (ATTN_HEADS, batch, HEAD_DIM))], axis=2)
    return pl.pallas_call(
        _attn_kernel,
        grid=(batch, ATTN_HEADS, nq),
        in_specs=[
            pl.BlockSpec((seq_len, HEAD_DIM), lambda b, h, qi: (b, h)),
            pl.BlockSpec((seq_len, HEAD_DIM), lambda b, h, qi: (b, k_off + h)),
            pl.BlockSpec((seq_len, HEAD_DIM), lambda b, h, qi: (b, h)),
            pl.BlockSpec((HEAD_ROWS, seq_len), lambda b, h, qi: (0, b)),
            pl.BlockSpec((None, None, 4, HEAD_DIM), lambda b, h, qi: (h, b, 0, 0)),
        ],
        out_specs=pl.BlockSpec((TQ, HEAD_DIM), lambda b, h, qi: (b * nq + qi, h)),
        out_shape=jax.ShapeDtypeStruct((m, ATTN_WIDTH), jnp.bfloat16),
        scratch_shapes=[
            pltpu.VMEM((seq_len, 2 * HEAD_DIM), jnp.bfloat16),
            pltpu.VMEM((2, TQ, TK), jnp.float32),
            pltpu.VMEM((TQ, HEAD_DIM), jnp.float32),
            pltpu.VMEM((TQ, 2 * HEAD_DIM), jnp.float32),
        ],
        compiler_params=pltpu.CompilerParams(
            dimension_semantics=("arbitrary", "arbitrary", "arbitrary"),
            vmem_limit_bytes=VMEM_LIMIT_BYTES),
        name="fox_attn",
    )(proj_qk, proj_qk, proj_vz, c_t, bounds)


def _out_proj_kernel(x_ref, a_ref, z_ref, l_ref, ga_ref, w_ref, gf_ref, o_ref):
    a = a_ref[...].astype(jnp.float32)
    z = z_ref[...].astype(jnp.float32)
    ma = (a * _rms_scale(a) * ga_ref[...] * (z * _sigmoid(z))).astype(jnp.bfloat16)
    o = (x_ref[...]
         + jnp.dot(ma, w_ref[0:ATTN_WIDTH, :], preferred_element_type=jnp.float32)
         + jnp.dot(l_ref[...], w_ref[ATTN_WIDTH:, :], preferred_element_type=jnp.float32))
    o_ref[...] = o * _rms_scale(o) * gf_ref[...]


def _out_proj(x2, attn, proj_vz, mixed_lru, g_attn, w_out, g_final):
    m, d = x2.shape
    z_blk = 1
    return pl.pallas_call(
        _out_proj_kernel,
        grid=(m // TM_OUT,),
        in_specs=[
            pl.BlockSpec((TM_OUT, d), lambda i: (i, 0)),
            pl.BlockSpec((TM_OUT, ATTN_WIDTH), lambda i: (i, 0)),
            pl.BlockSpec((TM_OUT, ATTN_WIDTH), lambda i: (i, z_blk)),
            pl.BlockSpec((TM_OUT, LRU_WIDTH), lambda i: (i, 0)),
            pl.BlockSpec((1, ATTN_WIDTH), lambda i: (0, 0)),
            pl.BlockSpec((ATTN_WIDTH + LRU_WIDTH, d), lambda i: (0, 0), pipeline_mode=pl.Buffered(1)),
            pl.BlockSpec((1, d), lambda i: (0, 0)),
        ],
        out_specs=pl.BlockSpec((TM_OUT, d), lambda i: (i, 0)),
        out_shape=jax.ShapeDtypeStruct((m, d), jnp.float32),
        compiler_params=pltpu.CompilerParams(
            dimension_semantics=("arbitrary",), vmem_limit_bytes=VMEM_LIMIT_BYTES),
        name="out_proj",
    )(x2, attn, proj_vz, mixed_lru, g_attn, w_out, g_final)


def _pack_w_in_kernel(wt_ref, o_ref):
    f_lo = 3 * ATTN_WIDTH
    f_hi = f_lo + ATTN_HEADS
    o_ref[:, 0:f_lo] = wt_ref[0:f_lo, :].T.astype(o_ref.dtype)
    o_ref[:, f_lo:] = wt_ref[f_hi:, :].T.astype(o_ref.dtype)


def _pack_w_in(w0t):
    n_in, d = w0t.shape
    return pl.pallas_call(
        _pack_w_in_kernel,
        grid=(d // TR_PACK,),
        in_specs=[pl.BlockSpec((n_in, TR_PACK), lambda i: (0, i))],
        out_specs=pl.BlockSpec((TR_PACK, N_MAIN), lambda i: (i, 0)),
        out_shape=jax.ShapeDtypeStruct((d, N_MAIN), jnp.bfloat16),
        compiler_params=pltpu.CompilerParams(
            dimension_semantics=("arbitrary",), vmem_limit_bytes=VMEM_LIMIT_BYTES),
        name="pack_w_in",
    )(w0t)


def kernel(x, norm_g, w_in, b_f, conv_w, conv_b, w_rg, b_rg, w_ig, b_ig, lru_lambda,
           attn_norm_g, lru_norm_g, w_out, final_norm_g):
    batch, seq_len, d = x.shape
    assert norm_g.shape[0] == 1, "single-layer trunk"
    assert seq_len % TM_IN == 0 and seq_len % TQ == 0 and seq_len % TM_OUT == 0
    f_lo = 3 * ATTN_WIDTH
    f_hi = f_lo + ATTN_HEADS
    bf16 = jnp.bfloat16

    w0t = w_in[0].T
    w_main = _pack_w_in(w0t)
    w_f = jnp.pad(w0t[f_lo:f_hi], ((0, HEAD_ROWS - ATTN_HEADS), (0, 0))).astype(bf16)
    bf_col = jnp.pad(b_f[0], (0, HEAD_ROWS - ATTN_HEADS)).reshape(HEAD_ROWS, 1)
    w_gate = jnp.concatenate([w_rg[0], w_ig[0]], axis=-1).astype(bf16)
    row = lambda v: v.reshape(1, -1)

    x2 = x.reshape(batch * seq_len, d)
    g_in = row(norm_g[0])
    proj_qk, c_t, h, norms = _in_proj_qk(x2, g_in, w_main, w_f, bf_col, seq_len)
    proj_vz, mixed_lru = _in_proj_lru(h, w_main, conv_w[0], row(conv_b[0]), w_gate,
                                      row(b_rg[0]), row(b_ig[0]), row(lru_lambda[0]),
                                      row(lru_norm_g[0]), seq_len)
    attn = _attention(proj_qk, proj_vz, c_t, norms, batch, seq_len)
    out = _out_proj(x2, attn, proj_vz, mixed_lru, row(attn_norm_g[0]), w_out[0].astype(bf16),
                    row(final_norm_g))
    return out.reshape(batch, seq_len, d)
```

```python
import functools

import jax
import jax.numpy as jnp
from jax import lax
from jax.experimental import pallas as pl
from jax.experimental.pallas import tpu as pltpu

ATTN_HEADS = 8
HEAD_DIM = 128
ATTN_WIDTH = ATTN_HEADS * HEAD_DIM
LRU_BLOCKS = 8
LRU_BLOCK_W = 128
LRU_WIDTH = LRU_BLOCKS * LRU_BLOCK_W
CONV_W = 4
LRU_C = 8.0
NORM_EPS = 1e-6
LOG2_E = 1.4426950408889634
Q_SCALE = HEAD_DIM ** -0.5 * LOG2_E
N_MAIN = 3 * ATTN_WIDTH + ATTN_WIDTH + 2 * LRU_WIDTH

SUBLANES_F32 = 8
HEAD_ROWS = 16
VMEM_LIMIT_BYTES = 58 * 1024 * 1024

TM_IN = 512
LRU_PHASES = 4
CUM_CHUNK = 256
TQ = 1024
TK = 512
TM_OUT = 512
TR_PACK = 256

_NEG_BIG = -1e30
SKIP_EXPONENT = 152.0


def _sigmoid_of_double(y):
    return 0.5 + 0.5 * jnp.tanh(y)


def _silu(x):
    hx = 0.5 * x
    return hx + hx * jnp.tanh(hx)


def _log_sigmoid(x):
    return jnp.minimum(x, 0.0) - jnp.log1p(jnp.exp(-jnp.abs(x)))


def _softplus(x):
    return jnp.maximum(x, 0.0) + jnp.log1p(jnp.exp(-jnp.abs(x)))


def _rms_scale(x32):
    return lax.rsqrt(jnp.mean(x32 * x32, axis=-1, keepdims=True) + NORM_EPS)


def _head_norm2_max(xb):
    x = xb.astype(jnp.float32)
    sq = x * x
    rows = []
    for hd in range(ATTN_HEADS):
        rs = jnp.sum(sq[:, hd * HEAD_DIM:(hd + 1) * HEAD_DIM], axis=1, keepdims=True)
        rows.append(jnp.broadcast_to(jnp.max(rs, axis=0, keepdims=True), (1, HEAD_DIM)))
    return jnp.concatenate(rows, axis=0)


def _in_proj_qk_kernel(x_ref, g_ref, w_ref, wf_ref, bf_ref, proj_ref, c_ref, h_ref, nrm_ref,
                       carry_ref, *, steps_per_seq):
    i = pl.program_id(0)
    x = x_ref[...]
    h = (x * _rms_scale(x) * g_ref[...]).astype(jnp.bfloat16)
    h_ref[...] = h

    q = jnp.dot(h, w_ref[:, 0:ATTN_WIDTH], preferred_element_type=jnp.float32)
    qb = (q * Q_SCALE).astype(proj_ref.dtype)
    proj_ref[:, 0:ATTN_WIDTH] = qb
    k = jnp.dot(h, w_ref[:, ATTN_WIDTH:], preferred_element_type=jnp.float32)
    kb = k.astype(proj_ref.dtype)
    proj_ref[:, ATTN_WIDTH:] = kb
    nrm_ref[...] = jnp.concatenate([_head_norm2_max(qb), _head_norm2_max(kb)], axis=0)

    f = lax.dot_general(wf_ref[...], h, (((1,), (1,)), ((), ())),
                        preferred_element_type=jnp.float32)
    logf = _log_sigmoid(f + bf_ref[...]) * LOG2_E

    @pl.when(i % steps_per_seq == 0)
    def _():
        carry_ref[...] = jnp.zeros_like(carry_ref)

    r = lax.broadcasted_iota(jnp.int32, (CUM_CHUNK, CUM_CHUNK), 0)
    c = lax.broadcasted_iota(jnp.int32, (CUM_CHUNK, CUM_CHUNK), 1)
    tri = jnp.where(r <= c, 1.0, 0.0).astype(jnp.bfloat16)
    carry = carry_ref[:, 0:1]
    for n in range(TM_IN // CUM_CHUNK):
        cols = slice(n * CUM_CHUNK, (n + 1) * CUM_CHUNK)
        v = logf[:, cols]
        hi = v.astype(jnp.bfloat16)
        r1 = v - hi.astype(jnp.float32)
        mid = r1.astype(jnp.bfloat16)
        lo = (r1 - mid.astype(jnp.float32)).astype(jnp.bfloat16)
        cs = (jnp.dot(hi, tri, preferred_element_type=jnp.float32)
              + jnp.dot(mid, tri, preferred_element_type=jnp.float32)
              + jnp.dot(lo, tri, preferred_element_type=jnp.float32)) + carry
        c_ref[:, cols] = cs
        carry = cs[:, CUM_CHUNK - 1:CUM_CHUNK]
    carry_ref[...] = jnp.broadcast_to(carry, carry_ref.shape)


def _in_proj_qk(x2, g, w_main, w_f, b_f, seq_len):
    m, d = x2.shape
    kern = functools.partial(_in_proj_qk_kernel, steps_per_seq=seq_len // TM_IN)
    return pl.pallas_call(
        kern,
        grid=(m // TM_IN,),
        in_specs=[
            pl.BlockSpec((TM_IN, d), lambda i: (i, 0)),
            pl.BlockSpec((1, d), lambda i: (0, 0)),
            pl.BlockSpec((d, 2 * ATTN_WIDTH), lambda i: (0, 0), pipeline_mode=pl.Buffered(1)),
            pl.BlockSpec((HEAD_ROWS, d), lambda i: (0, 0)),
            pl.BlockSpec((HEAD_ROWS, 1), lambda i: (0, 0)),
        ],
        out_specs=[
            pl.BlockSpec((TM_IN, 2 * ATTN_WIDTH), lambda i: (i, 0)),
            pl.BlockSpec((HEAD_ROWS, TM_IN), lambda i: (0, i)),
            pl.BlockSpec((TM_IN, d), lambda i: (i, 0)),
            pl.BlockSpec((None, 2 * ATTN_HEADS, HEAD_DIM), lambda i: (i, 0, 0)),
        ],
        out_shape=[
            jax.ShapeDtypeStruct((m, 2 * ATTN_WIDTH), jnp.bfloat16),
            jax.ShapeDtypeStruct((HEAD_ROWS, m), jnp.float32),
            jax.ShapeDtypeStruct((m, d), jnp.bfloat16),
            jax.ShapeDtypeStruct((m // TM_IN, 2 * ATTN_HEADS, HEAD_DIM), jnp.float32),
        ],
        scratch_shapes=[pltpu.VMEM((HEAD_ROWS, 128), jnp.float32)],
        compiler_params=pltpu.CompilerParams(
            dimension_semantics=("arbitrary",), vmem_limit_bytes=VMEM_LIMIT_BYTES),
        name="in_proj_qk",
    )(x2, g, w_main, w_f, b_f)


def _in_proj_lru_kernel(h_ref, wvz_ref, wl_ref, cw_ref, cb_ref, wg_ref, br_ref, bi_ref,
                        lam_ref, gl_ref, proj_ref, mix_ref, xs_ref, a_ref, b_ref, zl_ref, hc_ref,
                        *, steps_per_seq):
    i = pl.program_id(0)
    halo = SUBLANES_F32
    f32 = jnp.float32
    pw = LRU_WIDTH // LRU_PHASES

    @pl.when(i % steps_per_seq == 0)
    def _():
        xs_ref[0:halo, :] = jnp.zeros((halo, LRU_WIDTH), f32)
        hc_ref[...] = jnp.zeros_like(hc_ref)

    def project_x_lru(c0):
        xs_ref[halo:halo + TM_IN, c0:c0 + pw] = jnp.dot(
            h_ref[...], wl_ref[:, c0:c0 + pw], preferred_element_type=f32)

    def phase(j):
        c0 = j * pw
        cols = slice(c0, c0 + pw)
        if j + 1 < LRU_PHASES:
            project_x_lru(c0 + pw)
        vz = slice(2 * c0, 2 * (c0 + pw))
        proj_ref[:, vz] = jnp.dot(h_ref[...], wvz_ref[:, vz],
                                  preferred_element_type=f32).astype(proj_ref.dtype)
        zl_ref[:, cols] = jnp.dot(h_ref[...], wl_ref[:, LRU_WIDTH + c0:LRU_WIDTH + c0 + pw],
                                  preferred_element_type=f32)

        u = cb_ref[:, cols] + cw_ref[CONV_W - 1:CONV_W, cols] * xs_ref[halo:halo + TM_IN, cols]
        for k in range(CONV_W - 1):
            d = CONV_W - 1 - k
            u = u + cw_ref[k:k + 1, cols] * xs_ref[halo - d:halo - d + TM_IN, cols]
        xs_ref[0:halo, cols] = xs_ref[TM_IN:TM_IN + halo, cols]

        decay = LRU_C * _softplus(-lam_ref[:, cols])
        decay2 = -LOG2_E * decay
        ub = u.astype(jnp.bfloat16)
        for t in range(pw // LRU_BLOCK_W):
            lc = slice(t * LRU_BLOCK_W, (t + 1) * LRU_BLOCK_W)
            bc = slice(c0 + t * LRU_BLOCK_W, c0 + (t + 1) * LRU_BLOCK_W)
            gate = jnp.dot(ub[:, lc], wg_ref[c0 // LRU_BLOCK_W + t], preferred_element_type=f32)
            r = _sigmoid_of_double(gate[:, :LRU_BLOCK_W] + br_ref[:, bc])
            ig = _sigmoid_of_double(gate[:, LRU_BLOCK_W:] + bi_ref[:, bc])
            a = jnp.exp2(r * decay2[:, lc])
            a_ref[:, bc] = a
            one_minus_a2 = jnp.tanh(r * decay[:, lc]) * (1.0 + a * a)
            pos = one_minus_a2 > 0.0
            mult = jnp.where(pos, one_minus_a2 * lax.rsqrt(jnp.where(pos, one_minus_a2, 1.0)), 0.0)
            b_ref[:, bc] = mult * ig * u[:, lc]

        row = lax.broadcasted_iota(jnp.int32, (SUBLANES_F32, pw), 0)
        hprev = hc_ref[0:1, cols]
        for gidx in range(TM_IN // SUBLANES_F32):
            rows = slice(gidx * SUBLANES_F32, (gidx + 1) * SUBLANES_F32)
            av = a_ref[rows, cols]
            bv = b_ref[rows, cols]
            for d in (1, 2, 4):
                a_sh = jnp.where(row >= d, pltpu.roll(av, d, 0), 1.0)
                b_sh = jnp.where(row >= d, pltpu.roll(bv, d, 0), 0.0)
                bv = av * b_sh + bv
                av = av * a_sh
            hv = av * hprev + bv
            b_ref[rows, cols] = hv
            hprev = hv[SUBLANES_F32 - 1:SUBLANES_F32, :]
        hc_ref[0:1, cols] = hprev

    always = i >= 0
    project_x_lru(0)
    for j in range(LRU_PHASES):
        pl.when(always)(functools.partial(phase, j))

    hv = b_ref[...]
    z = zl_ref[...]
    y = hv * _rms_scale(hv) * gl_ref[...]
    mix_ref[...] = (y * _silu(z)).astype(mix_ref.dtype)


def _in_proj_lru(h, w_main, conv_w, conv_b, w_gate, b_rg, b_ig, lam, g_lru, seq_len):
    m, d = h.shape
    kern = functools.partial(_in_proj_lru_kernel, steps_per_seq=seq_len // TM_IN)
    vec = lambda: pl.BlockSpec((1, LRU_WIDTH), lambda i: (0, 0))
    wblk = lambda j: pl.BlockSpec((d, 2 * ATTN_WIDTH), lambda i: (0, j), pipeline_mode=pl.Buffered(1))
    return pl.pallas_call(
        kern,
        grid=(m // TM_IN,),
        in_specs=[
            pl.BlockSpec((TM_IN, d), lambda i: (i, 0)),
            wblk(1),
            wblk(2),
            pl.BlockSpec((CONV_W, LRU_WIDTH), lambda i: (0, 0)),
            vec(),
            pl.BlockSpec((LRU_BLOCKS, LRU_BLOCK_W, 2 * LRU_BLOCK_W), lambda i: (0, 0, 0)),
            vec(), vec(), vec(), vec(),
        ],
        out_specs=[
            pl.BlockSpec((TM_IN, 2 * ATTN_WIDTH), lambda i: (i, 0)),
            pl.BlockSpec((TM_IN, LRU_WIDTH), lambda i: (i, 0)),
        ],
        out_shape=[
            jax.ShapeDtypeStruct((m, 2 * ATTN_WIDTH), jnp.bfloat16),
            jax.ShapeDtypeStruct((m, LRU_WIDTH), jnp.bfloat16),
        ],
        scratch_shapes=[
            pltpu.VMEM((TM_IN + SUBLANES_F32, LRU_WIDTH), jnp.float32),
            pltpu.VMEM((TM_IN, LRU_WIDTH), jnp.float32),
            pltpu.VMEM((TM_IN, LRU_WIDTH), jnp.float32),
            pltpu.VMEM((TM_IN, LRU_WIDTH), jnp.float32),
            pltpu.VMEM((SUBLANES_F32, LRU_WIDTH), jnp.float32),
        ],
        compiler_params=pltpu.CompilerParams(
            dimension_semantics=("arbitrary",), vmem_limit_bytes=VMEM_LIMIT_BYTES),
        name="in_proj_lru",
    )(h, w_main, w_main, conv_w, conv_b, w_gate, b_rg, b_ig, lam, g_lru)


def _attn_kernel(q_ref, k_ref, v_ref, c_ref, bnd_ref, o_ref, vx_ref, s_ref, m_ref, acc_ref,
                 trips_ref):
    h = pl.program_id(1)
    qi = pl.program_id(2)
    nq = pl.num_programs(2)
    lanes = HEAD_DIM
    sub = TQ // TK
    all_rows = (0, TQ)

    def scores(tile, c, rows):
        r0, n = rows
        q0 = pl.multiple_of(tile * TQ, TQ)
        k0 = pl.multiple_of(c * TK, TK)
        s = lax.dot_general(q_ref[pl.ds(q0 + r0, n), :], k_ref[pl.ds(k0, TK), :],
                            (((1,), (1,)), ((), ())), preferred_element_type=jnp.float32)
        return s - c_ref[pl.ds(h, 1), pl.ds(k0, TK)]

    @pl.when(qi == 0)
    def _():
        vx_ref[:, 0:lanes] = v_ref[...]
        vx_ref[:, lanes:] = jnp.ones((vx_ref.shape[0], lanes), vx_ref.dtype)
        s_ref[0] = scores(0, 0, all_rows)

        ntile = trips_ref.shape[0]
        chunk = lax.broadcasted_iota(jnp.int32, (ntile, lanes), 1)
        n_left = lax.broadcasted_iota(jnp.int32, (ntile, lanes), 0) * sub
        c_end = bnd_ref[0:1, :]
        k_n2 = bnd_ref[1:2, :]
        c_q = bnd_ref[SUBLANES_F32:SUBLANES_F32 + ntile, :]
        q_n2 = bnd_ref[SUBLANES_F32 + ntile:SUBLANES_F32 + 2 * ntile, :]
        slack = (c_end - c_q) - SKIP_EXPONENT
        skippable = (slack > 0.0) & (slack * slack > 4.0 * q_n2 * k_n2)
        first_live = jnp.min(jnp.where(skippable | (chunk >= n_left), n_left, chunk),
                             axis=1, keepdims=True)
        live_pairs = n_left[:, 0:1] // 2 - first_live // 2
        for r in range(ntile):
            trips_ref[r] = live_pairs[r, 0]

    def step(c, slot, rows, masked, first, next_tile, next_c, next_rows):
        r0, n = rows
        s = s_ref[slot, r0:r0 + n, :]
        s_ref[1 - slot, next_rows[0]:next_rows[0] + next_rows[1], :] = scores(
            next_tile, next_c, next_rows)
        if masked:
            row = lax.broadcasted_iota(jnp.int32, (TK, TK), 0)
            col = lax.broadcasted_iota(jnp.int32, (TK, TK), 1)
            tri = jnp.where(col <= row, s[0:TK, :], _NEG_BIG)
            s = tri if n == TK else jnp.concatenate([tri, s[TK:, :]], axis=0)
        tiles = [s[:, i * lanes:(i + 1) * lanes] for i in range(TK // lanes)]
        m_cur = jnp.max(functools.reduce(jnp.maximum, tiles), axis=1, keepdims=True)
        k0 = pl.multiple_of(c * TK, TK)
        if first:
            m_new = jnp.broadcast_to(m_cur, (n, lanes))
        else:
            m_prev = m_ref[r0:r0 + n, :]
            m_new = jnp.maximum(m_prev, m_cur)
            alpha = jnp.exp2(m_prev - m_new)
        p = jnp.concatenate([jnp.exp2(t - m_new) for t in tiles], axis=1).astype(jnp.bfloat16)
        pv = jnp.dot(p, vx_ref[pl.ds(k0, TK), :], preferred_element_type=jnp.float32)
        if first:
            acc_ref[r0:r0 + n, :] = pv
        else:
            acc_ref[r0:r0 + n, 0:lanes] = alpha * acc_ref[r0:r0 + n, 0:lanes] + pv[:, 0:lanes]
            acc_ref[r0:r0 + n, lanes:] = alpha * acc_ref[r0:r0 + n, lanes:] + pv[:, lanes:]
        m_ref[r0:r0 + n, :] = m_new

    n_off = qi * sub
    trips = trips_ref[qi]

    nxt_tile = jnp.minimum(qi + 1, nq - 1)
    for si in range(sub):
        if si + 1 < sub:
            step(n_off + si, si % 2, (si * TK, TQ - si * TK), True, si == 0,
                 qi, n_off + si + 1, ((si + 1) * TK, TQ - (si + 1) * TK))
        else:
            more = trips > 0
            step(n_off + si, si % 2, (si * TK, TQ - si * TK), True, si == 0,
                 jnp.where(more, qi, nxt_tile), jnp.where(more, n_off - 1, nxt_tile * sub),
                 all_rows)

    def body(i, carry):
        hi = n_off - 1 - 2 * i
        more = i + 1 < trips
        step(hi, 0, all_rows, False, False, qi, hi - 1, all_rows)
        step(hi - 1, 1, all_rows, False, False,
             jnp.where(more, qi, nxt_tile), jnp.where(more, hi - 2, nxt_tile * sub), all_rows)
        return carry

    lax.fori_loop(0, trips, body, 0)
    o_ref[...] = (acc_ref[:, 0:lanes] / acc_ref[:, lanes:]).astype(o_ref.dtype)


def _attention(proj_qk, proj_vz, c_t, norms, batch, seq_len):
    assert TQ % TK == 0 and (TQ // TK) % 2 == 0
    assert TQ % TM_IN == 0
    m = proj_qk.shape[0]
    nq = seq_len // TQ
    nch = seq_len // TK
    assert nch <= HEAD_DIM
    k_off = ATTN_WIDTH // HEAD_DIM
    c3 = c_t[:ATTN_HEADS].reshape(ATTN_HEADS, batch, seq_len)
    n2 = norms[:, :, 0].reshape(batch, nq, TQ // TM_IN, 2 * ATTN_HEADS)
    qn = n2[..., :ATTN_HEADS].max(axis=2).transpose(2, 0, 1)
    kn = n2[..., ATTN_HEADS:].max(axis=(1, 2)).T
    lanes_of = lambda a: jnp.broadcast_to(a[..., None], a.shape + (HEAD_DIM,))
    c_end = jnp.pad(c3[:, :, TK - 1::TK], ((0, 0), (0, 0), (0, HEAD_DIM - nch)))
    head_rows = jnp.stack([c_end, lanes_of(kn)], axis=2)
    head_rows = jnp.pad(head_rows, ((0, 0), (0, 0), (0, SUBLANES_F32 - 2), (0, 0)))
    bounds = jnp.concatenate([head_rows, lanes_of(c3[:, :, ::TQ]), lanes_of(qn)], axis=2)
    return pl.pallas_call(
        _attn_kernel,
        grid=(batch, ATTN_HEADS, nq),
        in_specs=[
            pl.BlockSpec((seq_len, HEAD_DIM), lambda b, h, qi: (b, h)),
            pl.BlockSpec((seq_len, HEAD_DIM), lambda b, h, qi: (b, k_off + h)),
            pl.BlockSpec((seq_len, HEAD_DIM), lambda b, h, qi: (b, h)),
            pl.BlockSpec((HEAD_ROWS, seq_len), lambda b, h, qi: (0, b)),
            pl.BlockSpec((None, None, SUBLANES_F32 + 2 * nq, HEAD_DIM), lambda b, h, qi: (h, b, 0, 0)),
        ],
        out_specs=pl.BlockSpec((TQ, HEAD_DIM), lambda b, h, qi: (b * nq + qi, h)),
        out_shape=jax.ShapeDtypeStruct((m, ATTN_WIDTH), jnp.bfloat16),
        scratch_shapes=[
            pltpu.VMEM((seq_len, 2 * HEAD_DIM), jnp.bfloat16),
            pltpu.VMEM((2, TQ, TK), jnp.float32),
            pltpu.VMEM((TQ, HEAD_DIM), jnp.float32),
            pltpu.VMEM((TQ, 2 * HEAD_DIM), jnp.float32),
            pltpu.SMEM((nq,), jnp.int32),
        ],
        compiler_params=pltpu.CompilerParams(
            dimension_semantics=("arbitrary", "arbitrary", "arbitrary"),
            vmem_limit_bytes=VMEM_LIMIT_BYTES),
        name="fox_attn",
    )(proj_qk, proj_qk, proj_vz, c_t, bounds)


def _out_proj_kernel(x_ref, a_ref, z_ref, l_ref, ga_ref, w_ref, gf_ref, o_ref):
    a = a_ref[...].astype(jnp.float32)
    z = z_ref[...].astype(jnp.float32)
    ma = (a * _rms_scale(a) * ga_ref[...] * _silu(z)).astype(jnp.bfloat16)
    o = (x_ref[...]
         + jnp.dot(ma, w_ref[0:ATTN_WIDTH, :], preferred_element_type=jnp.float32)
         + jnp.dot(l_ref[...], w_ref[ATTN_WIDTH:, :], preferred_element_type=jnp.float32))
    o_ref[...] = o * _rms_scale(o) * gf_ref[...]


def _out_proj(x2, attn, proj_vz, mixed_lru, g_attn, w_out, g_final):
    m, d = x2.shape
    z_blk = 1
    return pl.pallas_call(
        _out_proj_kernel,
        grid=(m // TM_OUT,),
        in_specs=[
            pl.BlockSpec((TM_OUT, d), lambda i: (i, 0)),
            pl.BlockSpec((TM_OUT, ATTN_WIDTH), lambda i: (i, 0)),
            pl.BlockSpec((TM_OUT, ATTN_WIDTH), lambda i: (i, z_blk)),
            pl.BlockSpec((TM_OUT, LRU_WIDTH), lambda i: (i, 0)),
            pl.BlockSpec((1, ATTN_WIDTH), lambda i: (0, 0)),
            pl.BlockSpec((ATTN_WIDTH + LRU_WIDTH, d), lambda i: (0, 0), pipeline_mode=pl.Buffered(1)),
            pl.BlockSpec((1, d), lambda i: (0, 0)),
        ],
        out_specs=pl.BlockSpec((TM_OUT, d), lambda i: (i, 0)),
        out_shape=jax.ShapeDtypeStruct((m, d), jnp.float32),
        compiler_params=pltpu.CompilerParams(
            dimension_semantics=("arbitrary",), vmem_limit_bytes=VMEM_LIMIT_BYTES),
        name="out_proj",
    )(x2, attn, proj_vz, mixed_lru, g_attn, w_out, g_final)


def _pack_w_in_kernel(wt_ref, o_ref):
    f_lo = 3 * ATTN_WIDTH
    f_hi = f_lo + ATTN_HEADS
    o_ref[:, 0:f_lo] = wt_ref[0:f_lo, :].T.astype(o_ref.dtype)
    o_ref[:, f_lo:] = wt_ref[f_hi:, :].T.astype(o_ref.dtype)


def _pack_w_in(w0t):
    n_in, d = w0t.shape
    return pl.pallas_call(
        _pack_w_in_kernel,
        grid=(d // TR_PACK,),
        in_specs=[pl.BlockSpec((n_in, TR_PACK), lambda i: (0, i))],
        out_specs=pl.BlockSpec((TR_PACK, N_MAIN), lambda i: (i, 0)),
        out_shape=jax.ShapeDtypeStruct((d, N_MAIN), jnp.bfloat16),
        compiler_params=pltpu.CompilerParams(
            dimension_semantics=("arbitrary",), vmem_limit_bytes=VMEM_LIMIT_BYTES),
        name="pack_w_in",
    )(w0t)


def kernel(x, norm_g, w_in, b_f, conv_w, conv_b, w_rg, b_rg, w_ig, b_ig, lru_lambda,
           attn_norm_g, lru_norm_g, w_out, final_norm_g):
    batch, seq_len, d = x.shape
    assert norm_g.shape[0] == 1, "single-layer trunk"
    assert seq_len % TM_IN == 0 and seq_len % TQ == 0 and seq_len % TM_OUT == 0
    f_lo = 3 * ATTN_WIDTH
    f_hi = f_lo + ATTN_HEADS
    bf16 = jnp.bfloat16

    w0t = w_in[0].T
    w_main = _pack_w_in(w0t)
    w_f = jnp.pad(w0t[f_lo:f_hi], ((0, HEAD_ROWS - ATTN_HEADS), (0, 0))).astype(bf16)
    bf_col = jnp.pad(b_f[0], (0, HEAD_ROWS - ATTN_HEADS)).reshape(HEAD_ROWS, 1)
    w_gate = (0.5 * jnp.concatenate([w_rg[0], w_ig[0]], axis=-1)).astype(bf16)
    row = lambda v: v.reshape(1, -1)

    x2 = x.reshape(batch * seq_len, d)
    g_in = row(norm_g[0])
    proj_qk, c_t, h, norms = _in_proj_qk(x2, g_in, w_main, w_f, bf_col, seq_len)
    proj_vz, mixed_lru = _in_proj_lru(h, w_main, conv_w[0], row(conv_b[0]), w_gate,
                                      row(0.5 * b_rg[0]), row(0.5 * b_ig[0]), row(lru_lambda[0]),
                                      row(lru_norm_g[0]), seq_len)
    attn = _attention(proj_qk, proj_vz, c_t, norms, batch, seq_len)
    out = _out_proj(x2, attn, proj_vz, mixed_lru, row(attn_norm_g[0]), w_out[0].astype(bf16),
                    row(final_norm_g))
    return out.reshape(batch, seq_len, d)
```

```python
import functools

import jax
import jax.numpy as jnp
from jax import lax
from jax.experimental import pallas as pl
from jax.experimental.pallas import tpu as pltpu

ATTN_HEADS = 8
HEAD_DIM = 128
ATTN_WIDTH = ATTN_HEADS * HEAD_DIM
LRU_BLOCKS = 8
LRU_BLOCK_W = 128
LRU_WIDTH = LRU_BLOCKS * LRU_BLOCK_W
CONV_W = 4
LRU_C = 8.0
NORM_EPS = 1e-6
LOG2_E = 1.4426950408889634
Q_SCALE = HEAD_DIM ** -0.5 * LOG2_E
N_MAIN = 3 * ATTN_WIDTH + ATTN_WIDTH + 2 * LRU_WIDTH

SUBLANES_F32 = 8
HEAD_ROWS = 16
VMEM_LIMIT_BYTES = 58 * 1024 * 1024

TM_IN = 512
LRU_PHASES = 4
CUM_CHUNK = 256
TQ = 1024
TK = 512
TM_OUT = 512
TR_PACK = 256

_NEG_BIG = -1e30
SKIP_EXPONENT = 152.0


def _sigmoid_of_double(y):
    return 0.5 + 0.5 * jnp.tanh(y)


def _silu(x):
    hx = 0.5 * x
    return hx + hx * jnp.tanh(hx)


def _log_sigmoid(x):
    return jnp.minimum(x, 0.0) - jnp.log1p(jnp.exp(-jnp.abs(x)))


def _softplus(x):
    return jnp.maximum(x, 0.0) + jnp.log1p(jnp.exp(-jnp.abs(x)))


def _rms_scale(x32):
    return lax.rsqrt(jnp.mean(x32 * x32, axis=-1, keepdims=True) + NORM_EPS)


def _head_norm2_max(xb):
    x = xb.astype(jnp.float32)
    sq = x * x
    rows = []
    for hd in range(ATTN_HEADS):
        rs = jnp.sum(sq[:, hd * HEAD_DIM:(hd + 1) * HEAD_DIM], axis=1, keepdims=True)
        rows.append(jnp.broadcast_to(jnp.max(rs, axis=0, keepdims=True), (1, HEAD_DIM)))
    return jnp.concatenate(rows, axis=0)


def _in_proj_qk_kernel(x_ref, g_ref, w_ref, wf_ref, bf_ref, proj_ref, c_ref, h_ref, nrm_ref,
                       carry_ref, *, steps_per_seq):
    i = pl.program_id(0)
    x = x_ref[...]
    h = (x * _rms_scale(x) * g_ref[...]).astype(jnp.bfloat16)
    h_ref[...] = h

    q = jnp.dot(h, w_ref[:, 0:ATTN_WIDTH], preferred_element_type=jnp.float32)
    qb = (q * Q_SCALE).astype(proj_ref.dtype)
    proj_ref[:, 0:ATTN_WIDTH] = qb
    k = jnp.dot(h, w_ref[:, ATTN_WIDTH:], preferred_element_type=jnp.float32)
    kb = k.astype(proj_ref.dtype)
    proj_ref[:, ATTN_WIDTH:] = kb
    nrm_ref[...] = jnp.concatenate([_head_norm2_max(qb), _head_norm2_max(kb)], axis=0)

    f = lax.dot_general(wf_ref[...], h, (((1,), (1,)), ((), ())),
                        preferred_element_type=jnp.float32)
    logf = _log_sigmoid(f + bf_ref[...]) * LOG2_E

    @pl.when(i % steps_per_seq == 0)
    def _():
        carry_ref[...] = jnp.zeros_like(carry_ref)

    r = lax.broadcasted_iota(jnp.int32, (CUM_CHUNK, CUM_CHUNK), 0)
    c = lax.broadcasted_iota(jnp.int32, (CUM_CHUNK, CUM_CHUNK), 1)
    tri = jnp.where(r <= c, 1.0, 0.0).astype(jnp.bfloat16)
    carry = carry_ref[:, 0:1]
    for n in range(TM_IN // CUM_CHUNK):
        cols = slice(n * CUM_CHUNK, (n + 1) * CUM_CHUNK)
        v = logf[:, cols]
        hi = v.astype(jnp.bfloat16)
        r1 = v - hi.astype(jnp.float32)
        mid = r1.astype(jnp.bfloat16)
        lo = (r1 - mid.astype(jnp.float32)).astype(jnp.bfloat16)
        cs = (jnp.dot(hi, tri, preferred_element_type=jnp.float32)
              + jnp.dot(mid, tri, preferred_element_type=jnp.float32)
              + jnp.dot(lo, tri, preferred_element_type=jnp.float32)) + carry
        c_ref[:, cols] = cs
        carry = cs[:, CUM_CHUNK - 1:CUM_CHUNK]
    carry_ref[...] = jnp.broadcast_to(carry, carry_ref.shape)


def _in_proj_qk(x2, g, w_main, w_f, b_f, seq_len):
    m, d = x2.shape
    kern = functools.partial(_in_proj_qk_kernel, steps_per_seq=seq_len // TM_IN)
    return pl.pallas_call(
        kern,
        grid=(m // TM_IN,),
        in_specs=[
            pl.BlockSpec((TM_IN, d), lambda i: (i, 0)),
            pl.BlockSpec((1, d), lambda i: (0, 0)),
            pl.BlockSpec((d, 2 * ATTN_WIDTH), lambda i: (0, 0), pipeline_mode=pl.Buffered(1)),
            pl.BlockSpec((HEAD_ROWS, d), lambda i: (0, 0)),
            pl.BlockSpec((HEAD_ROWS, 1), lambda i: (0, 0)),
        ],
        out_specs=[
            pl.BlockSpec((TM_IN, 2 * ATTN_WIDTH), lambda i: (i, 0)),
            pl.BlockSpec((HEAD_ROWS, TM_IN), lambda i: (0, i)),
            pl.BlockSpec((TM_IN, d), lambda i: (i, 0)),
            pl.BlockSpec((None, 2 * ATTN_HEADS, HEAD_DIM), lambda i: (i, 0, 0)),
        ],
        out_shape=[
            jax.ShapeDtypeStruct((m, 2 * ATTN_WIDTH), jnp.bfloat16),
            jax.ShapeDtypeStruct((HEAD_ROWS, m), jnp.float32),
            jax.ShapeDtypeStruct((m, d), jnp.bfloat16),
            jax.ShapeDtypeStruct((m // TM_IN, 2 * ATTN_HEADS, HEAD_DIM), jnp.float32),
        ],
        scratch_shapes=[pltpu.VMEM((HEAD_ROWS, 128), jnp.float32)],
        compiler_params=pltpu.CompilerParams(
            dimension_semantics=("arbitrary",), vmem_limit_bytes=VMEM_LIMIT_BYTES),
        name="in_proj_qk",
    )(x2, g, w_main, w_f, b_f)


def _in_proj_lru_kernel(h_ref, wvz_ref, wl_ref, cw_ref, cb_ref, wg_ref, br_ref, bi_ref,
                        lam_ref, gl_ref, proj_ref, mix_ref, xs_ref, a_ref, b_ref, zl_ref, hc_ref,
                        *, steps_per_seq):
    i = pl.program_id(0)
    halo = SUBLANES_F32
    f32 = jnp.float32
    pw = LRU_WIDTH // LRU_PHASES

    @pl.when(i % steps_per_seq == 0)
    def _():
        xs_ref[0:halo, :] = jnp.zeros((halo, LRU_WIDTH), f32)
        hc_ref[...] = jnp.zeros_like(hc_ref)

    def project_x_lru(c0):
        xs_ref[halo:halo + TM_IN, c0:c0 + pw] = jnp.dot(
            h_ref[...], wl_ref[:, c0:c0 + pw], preferred_element_type=f32)

    def phase(j):
        c0 = j * pw
        cols = slice(c0, c0 + pw)
        if j + 1 < LRU_PHASES:
            project_x_lru(c0 + pw)
        vz = slice(2 * c0, 2 * (c0 + pw))
        proj_ref[:, vz] = jnp.dot(h_ref[...], wvz_ref[:, vz],
                                  preferred_element_type=f32).astype(proj_ref.dtype)
        zl_ref[:, cols] = jnp.dot(h_ref[...], wl_ref[:, LRU_WIDTH + c0:LRU_WIDTH + c0 + pw],
                                  preferred_element_type=f32)

        u = cb_ref[:, cols] + cw_ref[CONV_W - 1:CONV_W, cols] * xs_ref[halo:halo + TM_IN, cols]
        for k in range(CONV_W - 1):
            d = CONV_W - 1 - k
            u = u + cw_ref[k:k + 1, cols] * xs_ref[halo - d:halo - d + TM_IN, cols]
        xs_ref[0:halo, cols] = xs_ref[TM_IN:TM_IN + halo, cols]

        decay = LRU_C * _softplus(-lam_ref[:, cols])
        decay2 = -LOG2_E * decay
        ub = u.astype(jnp.bfloat16)
        for t in range(pw // LRU_BLOCK_W):
            lc = slice(t * LRU_BLOCK_W, (t + 1) * LRU_BLOCK_W)
            bc = slice(c0 + t * LRU_BLOCK_W, c0 + (t + 1) * LRU_BLOCK_W)
            gate = jnp.dot(ub[:, lc], wg_ref[c0 // LRU_BLOCK_W + t], preferred_element_type=f32)
            r = _sigmoid_of_double(gate[:, :LRU_BLOCK_W] + br_ref[:, bc])
            ig = _sigmoid_of_double(gate[:, LRU_BLOCK_W:] + bi_ref[:, bc])
            a = jnp.exp2(r * decay2[:, lc])
            a_ref[:, bc] = a
            one_minus_a2 = jnp.tanh(r * decay[:, lc]) * (1.0 + a * a)
            pos = one_minus_a2 > 0.0
            mult = jnp.where(pos, one_minus_a2 * lax.rsqrt(jnp.where(pos, one_minus_a2, 1.0)), 0.0)
            b_ref[:, bc] = mult * ig * u[:, lc]

        row = lax.broadcasted_iota(jnp.int32, (SUBLANES_F32, pw), 0)
        hprev = hc_ref[0:1, cols]
        for gidx in range(TM_IN // SUBLANES_F32):
            rows = slice(gidx * SUBLANES_F32, (gidx + 1) * SUBLANES_F32)
            av = a_ref[rows, cols]
            bv = b_ref[rows, cols]
            for d in (1, 2, 4):
                a_sh = jnp.where(row >= d, pltpu.roll(av, d, 0), 1.0)
                b_sh = jnp.where(row >= d, pltpu.roll(bv, d, 0), 0.0)
                bv = av * b_sh + bv
                av = av * a_sh
            hv = av * hprev + bv
            b_ref[rows, cols] = hv
            hprev = hv[SUBLANES_F32 - 1:SUBLANES_F32, :]
        hc_ref[0:1, cols] = hprev

    always = i >= 0
    project_x_lru(0)
    for j in range(LRU_PHASES):
        pl.when(always)(functools.partial(phase, j))

    hv = b_ref[...]
    z = zl_ref[...]
    y = hv * _rms_scale(hv) * gl_ref[...]
    mix_ref[...] = (y * _silu(z)).astype(mix_ref.dtype)


def _in_proj_lru(h, w_main, conv_w, conv_b, w_gate, b_rg, b_ig, lam, g_lru, seq_len):
    m, d = h.shape
    kern = functools.partial(_in_proj_lru_kernel, steps_per_seq=seq_len // TM_IN)
    vec = lambda: pl.BlockSpec((1, LRU_WIDTH), lambda i: (0, 0))
    wblk = lambda j: pl.BlockSpec((d, 2 * ATTN_WIDTH), lambda i: (0, j), pipeline_mode=pl.Buffered(1))
    return pl.pallas_call(
        kern,
        grid=(m // TM_IN,),
        in_specs=[
            pl.BlockSpec((TM_IN, d), lambda i: (i, 0)),
            wblk(1),
            wblk(2),
            pl.BlockSpec((CONV_W, LRU_WIDTH), lambda i: (0, 0)),
            vec(),
            pl.BlockSpec((LRU_BLOCKS, LRU_BLOCK_W, 2 * LRU_BLOCK_W), lambda i: (0, 0, 0)),
            vec(), vec(), vec(), vec(),
        ],
        out_specs=[
            pl.BlockSpec((TM_IN, 2 * ATTN_WIDTH), lambda i: (i, 0)),
            pl.BlockSpec((TM_IN, LRU_WIDTH), lambda i: (i, 0)),
        ],
        out_shape=[
            jax.ShapeDtypeStruct((m, 2 * ATTN_WIDTH), jnp.bfloat16),
            jax.ShapeDtypeStruct((m, LRU_WIDTH), jnp.bfloat16),
        ],
        scratch_shapes=[
            pltpu.VMEM((TM_IN + SUBLANES_F32, LRU_WIDTH), jnp.float32),
            pltpu.VMEM((TM_IN, LRU_WIDTH), jnp.float32),
            pltpu.VMEM((TM_IN, LRU_WIDTH), jnp.float32),
            pltpu.VMEM((TM_IN, LRU_WIDTH), jnp.float32),
            pltpu.VMEM((SUBLANES_F32, LRU_WIDTH), jnp.float32),
        ],
        compiler_params=pltpu.CompilerParams(
            dimension_semantics=("arbitrary",), vmem_limit_bytes=VMEM_LIMIT_BYTES),
        name="in_proj_lru",
    )(h, w_main, w_main, conv_w, conv_b, w_gate, b_rg, b_ig, lam, g_lru)


def _attn_kernel(q_ref, k_ref, v_ref, c_ref, bnd_ref, o_ref, vx_ref, s_ref, m_ref, acc_ref,
                 trips_ref):
    h = pl.program_id(1)
    nq = trips_ref.shape[0]
    lanes = HEAD_DIM
    sub = TQ // TK
    all_rows = (0, TQ)

    def scores(tile, c, rows):
        r0, n = rows
        q0 = pl.multiple_of(tile * TQ, TQ)
        k0 = pl.multiple_of(c * TK, TK)
        s = lax.dot_general(q_ref[pl.ds(q0 + r0, n), :], k_ref[pl.ds(k0, TK), :],
                            (((1,), (1,)), ((), ())), preferred_element_type=jnp.float32)
        return s - c_ref[pl.ds(h, 1), pl.ds(k0, TK)]

    vx_ref[:, 0:lanes] = v_ref[...]
    vx_ref[:, lanes:] = jnp.ones((vx_ref.shape[0], lanes), vx_ref.dtype)
    s_ref[0] = scores(0, 0, all_rows)

    chunk = lax.broadcasted_iota(jnp.int32, (nq, lanes), 1)
    n_left = lax.broadcasted_iota(jnp.int32, (nq, lanes), 0) * sub
    c_end = bnd_ref[0:1, :]
    k_n2 = bnd_ref[1:2, :]
    c_q = bnd_ref[SUBLANES_F32:SUBLANES_F32 + nq, :]
    q_n2 = bnd_ref[SUBLANES_F32 + nq:SUBLANES_F32 + 2 * nq, :]
    slack = (c_end - c_q) - SKIP_EXPONENT
    skippable = (slack > 0.0) & (slack * slack > 4.0 * q_n2 * k_n2)
    first_live = jnp.min(jnp.where(skippable | (chunk >= n_left), n_left, chunk),
                         axis=1, keepdims=True)
    live_pairs = n_left[:, 0:1] // 2 - first_live // 2
    for r in range(nq):
        trips_ref[r] = live_pairs[r, 0]

    def step(c, slot, rows, masked, first, next_tile, next_c, next_rows):
        r0, n = rows
        s = s_ref[slot, r0:r0 + n, :]
        s_ref[1 - slot, next_rows[0]:next_rows[0] + next_rows[1], :] = scores(
            next_tile, next_c, next_rows)
        if masked:
            row = lax.broadcasted_iota(jnp.int32, (TK, TK), 0)
            col = lax.broadcasted_iota(jnp.int32, (TK, TK), 1)
            tri = jnp.where(col <= row, s[0:TK, :], _NEG_BIG)
            s = tri if n == TK else jnp.concatenate([tri, s[TK:, :]], axis=0)
        tiles = [s[:, i * lanes:(i + 1) * lanes] for i in range(TK // lanes)]
        m_cur = jnp.max(functools.reduce(jnp.maximum, tiles), axis=1, keepdims=True)
        k0 = pl.multiple_of(c * TK, TK)
        if first:
            m_new = jnp.broadcast_to(m_cur, (n, lanes))
        else:
            m_prev = m_ref[r0:r0 + n, :]
            m_new = jnp.maximum(m_prev, m_cur)
            alpha = jnp.exp2(m_prev - m_new)
        p = jnp.concatenate([jnp.exp2(t - m_new) for t in tiles], axis=1).astype(jnp.bfloat16)
        pv = jnp.dot(p, vx_ref[pl.ds(k0, TK), :], preferred_element_type=jnp.float32)
        if first:
            acc_ref[r0:r0 + n, :] = pv
        else:
            acc_ref[r0:r0 + n, 0:lanes] = alpha * acc_ref[r0:r0 + n, 0:lanes] + pv[:, 0:lanes]
            acc_ref[r0:r0 + n, lanes:] = alpha * acc_ref[r0:r0 + n, lanes:] + pv[:, lanes:]
        m_ref[r0:r0 + n, :] = m_new

    def tile(qi, carry):
        n_off = qi * sub
        trips = trips_ref[qi]

        nxt_tile = jnp.minimum(qi + 1, nq - 1)
        for si in range(sub):
            if si + 1 < sub:
                step(n_off + si, si % 2, (si * TK, TQ - si * TK), True, si == 0,
                     qi, n_off + si + 1, ((si + 1) * TK, TQ - (si + 1) * TK))
            else:
                more = trips > 0
                step(n_off + si, si % 2, (si * TK, TQ - si * TK), True, si == 0,
                     jnp.where(more, qi, nxt_tile), jnp.where(more, n_off - 1, nxt_tile * sub),
                     all_rows)

        def body(i, carry):
            hi = n_off - 1 - 2 * i
            more = i + 1 < trips
            step(hi, 0, all_rows, False, False, qi, hi - 1, all_rows)
            step(hi - 1, 1, all_rows, False, False,
                 jnp.where(more, qi, nxt_tile), jnp.where(more, hi - 2, nxt_tile * sub), all_rows)
            return carry

        lax.fori_loop(0, trips, body, 0)
        o_ref[pl.ds(pl.multiple_of(qi * TQ, TQ), TQ), :] = (
            acc_ref[:, 0:lanes] / acc_ref[:, lanes:]).astype(o_ref.dtype)
        return carry

    lax.fori_loop(0, nq, tile, 0)


def _attention(proj_qk, proj_vz, c_t, norms, batch, seq_len):
    assert TQ % TK == 0 and (TQ // TK) % 2 == 0
    assert TQ % TM_IN == 0
    m = proj_qk.shape[0]
    nq = seq_len // TQ
    nch = seq_len // TK
    assert nch <= HEAD_DIM
    k_off = ATTN_WIDTH // HEAD_DIM
    c3 = c_t[:ATTN_HEADS].reshape(ATTN_HEADS, batch, seq_len)
    n2 = norms[:, :, 0].reshape(batch, nq, TQ // TM_IN, 2 * ATTN_HEADS)
    qn = n2[..., :ATTN_HEADS].max(axis=2).transpose(2, 0, 1)
    kn = n2[..., ATTN_HEADS:].max(axis=(1, 2)).T
    lanes_of = lambda a: jnp.broadcast_to(a[..., None], a.shape + (HEAD_DIM,))
    c_end = jnp.pad(c3[:, :, TK - 1::TK], ((0, 0), (0, 0), (0, HEAD_DIM - nch)))
    head_rows = jnp.stack([c_end, lanes_of(kn)], axis=2)
    head_rows = jnp.pad(head_rows, ((0, 0), (0, 0), (0, SUBLANES_F32 - 2), (0, 0)))
    bounds = jnp.concatenate([head_rows, lanes_of(c3[:, :, ::TQ]), lanes_of(qn)], axis=2)
    return pl.pallas_call(
        _attn_kernel,
        grid=(batch, ATTN_HEADS),
        in_specs=[
            pl.BlockSpec((seq_len, HEAD_DIM), lambda b, h: (b, h)),
            pl.BlockSpec((seq_len, HEAD_DIM), lambda b, h: (b, k_off + h)),
            pl.BlockSpec((seq_len, HEAD_DIM), lambda b, h: (b, h)),
            pl.BlockSpec((HEAD_ROWS, seq_len), lambda b, h: (0, b)),
            pl.BlockSpec((None, None, SUBLANES_F32 + 2 * nq, HEAD_DIM), lambda b, h: (h, b, 0, 0)),
        ],
        out_specs=pl.BlockSpec((seq_len, HEAD_DIM), lambda b, h: (b, h)),
        out_shape=jax.ShapeDtypeStruct((m, ATTN_WIDTH), jnp.bfloat16),
        scratch_shapes=[
            pltpu.VMEM((seq_len, 2 * HEAD_DIM), jnp.bfloat16),
            pltpu.VMEM((2, TQ, TK), jnp.float32),
            pltpu.VMEM((TQ, HEAD_DIM), jnp.float32),
            pltpu.VMEM((TQ, 2 * HEAD_DIM), jnp.float32),
            pltpu.SMEM((nq,), jnp.int32),
        ],
        compiler_params=pltpu.CompilerParams(
            dimension_semantics=("arbitrary", "arbitrary"),
            vmem_limit_bytes=VMEM_LIMIT_BYTES),
        name="fox_attn",
    )(proj_qk, proj_qk, proj_vz, c_t, bounds)


def _out_proj_kernel(x_ref, a_ref, z_ref, l_ref, ga_ref, w_ref, gf_ref, o_ref):
    a = a_ref[...].astype(jnp.float32)
    z = z_ref[...].astype(jnp.float32)
    ma = (a * _rms_scale(a) * ga_ref[...] * _silu(z)).astype(jnp.bfloat16)
    o = (x_ref[...]
         + jnp.dot(ma, w_ref[0:ATTN_WIDTH, :], preferred_element_type=jnp.float32)
         + jnp.dot(l_ref[...], w_ref[ATTN_WIDTH:, :], preferred_element_type=jnp.float32))
    o_ref[...] = o * _rms_scale(o) * gf_ref[...]


def _out_proj(x2, attn, proj_vz, mixed_lru, g_attn, w_out, g_final):
    m, d = x2.shape
    z_blk = 1
    return pl.pallas_call(
        _out_proj_kernel,
        grid=(m // TM_OUT,),
        in_specs=[
            pl.BlockSpec((TM_OUT, d), lambda i: (i, 0)),
            pl.BlockSpec((TM_OUT, ATTN_WIDTH), lambda i: (i, 0)),
            pl.BlockSpec((TM_OUT, ATTN_WIDTH), lambda i: (i, z_blk)),
            pl.BlockSpec((TM_OUT, LRU_WIDTH), lambda i: (i, 0)),
            pl.BlockSpec((1, ATTN_WIDTH), lambda i: (0, 0)),
            pl.BlockSpec((ATTN_WIDTH + LRU_WIDTH, d), lambda i: (0, 0), pipeline_mode=pl.Buffered(1)),
            pl.BlockSpec((1, d), lambda i: (0, 0)),
        ],
        out_specs=pl.BlockSpec((TM_OUT, d), lambda i: (i, 0)),
        out_shape=jax.ShapeDtypeStruct((m, d), jnp.float32),
        compiler_params=pltpu.CompilerParams(
            dimension_semantics=("arbitrary",), vmem_limit_bytes=VMEM_LIMIT_BYTES),
        name="out_proj",
    )(x2, attn, proj_vz, mixed_lru, g_attn, w_out, g_final)


def _pack_w_in_kernel(wt_ref, o_ref):
    f_lo = 3 * ATTN_WIDTH
    f_hi = f_lo + ATTN_HEADS
    o_ref[:, 0:f_lo] = wt_ref[0:f_lo, :].T.astype(o_ref.dtype)
    o_ref[:, f_lo:] = wt_ref[f_hi:, :].T.astype(o_ref.dtype)


def _pack_w_in(w0t):
    n_in, d = w0t.shape
    return pl.pallas_call(
        _pack_w_in_kernel,
        grid=(d // TR_PACK,),
        in_specs=[pl.BlockSpec((n_in, TR_PACK), lambda i: (0, i))],
        out_specs=pl.BlockSpec((TR_PACK, N_MAIN), lambda i: (i, 0)),
        out_shape=jax.ShapeDtypeStruct((d, N_MAIN), jnp.bfloat16),
        compiler_params=pltpu.CompilerParams(
            dimension_semantics=("arbitrary",), vmem_limit_bytes=VMEM_LIMIT_BYTES),
        name="pack_w_in",
    )(w0t)


def kernel(x, norm_g, w_in, b_f, conv_w, conv_b, w_rg, b_rg, w_ig, b_ig, lru_lambda,
           attn_norm_g, lru_norm_g, w_out, final_norm_g):
    batch, seq_len, d = x.shape
    assert norm_g.shape[0] == 1, "single-layer trunk"
    assert seq_len % TM_IN == 0 and seq_len % TQ == 0 and seq_len % TM_OUT == 0
    f_lo = 3 * ATTN_WIDTH
    f_hi = f_lo + ATTN_HEADS
    bf16 = jnp.bfloat16

    w0t = w_in[0].T
    w_main = _pack_w_in(w0t)
    w_f = jnp.pad(w0t[f_lo:f_hi], ((0, HEAD_ROWS - ATTN_HEADS), (0, 0))).astype(bf16)
    bf_col = jnp.pad(b_f[0], (0, HEAD_ROWS - ATTN_HEADS)).reshape(HEAD_ROWS, 1)
    w_gate = (0.5 * jnp.concatenate([w_rg[0], w_ig[0]], axis=-1)).astype(bf16)
    row = lambda v: v.reshape(1, -1)

    x2 = x.reshape(batch * seq_len, d)
    g_in = row(norm_g[0])
    proj_qk, c_t, h, norms = _in_proj_qk(x2, g_in, w_main, w_f, bf_col, seq_len)
    proj_vz, mixed_lru = _in_proj_lru(h, w_main, conv_w[0], row(conv_b[0]), w_gate,
                                      row(0.5 * b_rg[0]), row(0.5 * b_ig[0]), row(lru_lambda[0]),
                                      row(lru_norm_g[0]), seq_len)
    attn = _attention(proj_qk, proj_vz, c_t, norms, batch, seq_len)
    out = _out_proj(x2, attn, proj_vz, mixed_lru, row(attn_norm_g[0]), w_out[0].astype(bf16),
                    row(final_norm_g))
    return out.reshape(batch, seq_len, d)
```

```python
import functools

import jax
import jax.numpy as jnp
from jax import lax
from jax.experimental import pallas as pl
from jax.experimental.pallas import tpu as pltpu

ATTN_HEADS = 8
HEAD_DIM = 128
ATTN_WIDTH = ATTN_HEADS * HEAD_DIM
LRU_BLOCKS = 8
LRU_BLOCK_W = 128
LRU_WIDTH = LRU_BLOCKS * LRU_BLOCK_W
CONV_W = 4
LRU_C = 8.0
NORM_EPS = 1e-6
LOG2_E = 1.4426950408889634
Q_SCALE = HEAD_DIM ** -0.5 * LOG2_E
N_MAIN = 3 * ATTN_WIDTH + ATTN_WIDTH + 2 * LRU_WIDTH

SUBLANES_F32 = 8
HEAD_ROWS = 16
VMEM_LIMIT_BYTES = 58 * 1024 * 1024

TM_QK = 1024
TM_IN = 512
LRU_PHASES = 4
CUM_CHUNK = 256
TQ = 1024
TK = 512
TM_OUT = 512
TR_PACK = 256

_NEG_BIG = -1e30
SKIP_EXPONENT = 152.0


def _sigmoid_of_double(y):
    return 0.5 + 0.5 * jnp.tanh(y)


def _silu(x):
    hx = 0.5 * x
    return hx + hx * jnp.tanh(hx)


def _log_sigmoid(x):
    return jnp.minimum(x, 0.0) - jnp.log1p(jnp.exp(-jnp.abs(x)))


def _softplus(x):
    return jnp.maximum(x, 0.0) + jnp.log1p(jnp.exp(-jnp.abs(x)))


def _rms_scale(x32):
    return lax.rsqrt(jnp.mean(x32 * x32, axis=-1, keepdims=True) + NORM_EPS)


def _head_norm2_max(xb):
    x = xb.astype(jnp.float32)
    sq = x * x
    rows = []
    for hd in range(ATTN_HEADS):
        rs = jnp.sum(sq[:, hd * HEAD_DIM:(hd + 1) * HEAD_DIM], axis=1, keepdims=True)
        rows.append(jnp.broadcast_to(jnp.max(rs, axis=0, keepdims=True), (1, HEAD_DIM)))
    return jnp.concatenate(rows, axis=0)


def _in_proj_qk_kernel(x_ref, g_ref, w_ref, wf_ref, bf_ref, proj_ref, c_ref, h_ref, nrm_ref,
                       carry_ref, *, steps_per_seq):
    i = pl.program_id(0)
    x = x_ref[...]
    h = (x * _rms_scale(x) * g_ref[...]).astype(jnp.bfloat16)
    h_ref[...] = h

    q = jnp.dot(h, w_ref[:, 0:ATTN_WIDTH], preferred_element_type=jnp.float32)
    qb = (q * Q_SCALE).astype(proj_ref.dtype)
    proj_ref[:, 0:ATTN_WIDTH] = qb
    k = jnp.dot(h, w_ref[:, ATTN_WIDTH:], preferred_element_type=jnp.float32)
    kb = k.astype(proj_ref.dtype)
    proj_ref[:, ATTN_WIDTH:] = kb
    nrm_ref[...] = jnp.concatenate([_head_norm2_max(qb), _head_norm2_max(kb)], axis=0)

    f = lax.dot_general(wf_ref[...], h, (((1,), (1,)), ((), ())),
                        preferred_element_type=jnp.float32)
    logf = _log_sigmoid(f + bf_ref[...]) * LOG2_E

    @pl.when(i % steps_per_seq == 0)
    def _():
        carry_ref[...] = jnp.zeros_like(carry_ref)

    r = lax.broadcasted_iota(jnp.int32, (CUM_CHUNK, CUM_CHUNK), 0)
    c = lax.broadcasted_iota(jnp.int32, (CUM_CHUNK, CUM_CHUNK), 1)
    tri = jnp.where(r <= c, 1.0, 0.0).astype(jnp.bfloat16)
    carry = carry_ref[:, 0:1]
    for n in range(TM_QK // CUM_CHUNK):
        cols = slice(n * CUM_CHUNK, (n + 1) * CUM_CHUNK)
        v = logf[:, cols]
        hi = v.astype(jnp.bfloat16)
        r1 = v - hi.astype(jnp.float32)
        mid = r1.astype(jnp.bfloat16)
        lo = (r1 - mid.astype(jnp.float32)).astype(jnp.bfloat16)
        cs = (jnp.dot(hi, tri, preferred_element_type=jnp.float32)
              + jnp.dot(mid, tri, preferred_element_type=jnp.float32)
              + jnp.dot(lo, tri, preferred_element_type=jnp.float32)) + carry
        c_ref[:, cols] = cs
        carry = cs[:, CUM_CHUNK - 1:CUM_CHUNK]
    carry_ref[...] = jnp.broadcast_to(carry, carry_ref.shape)


def _in_proj_qk(x2, g, w_main, w_f, b_f, seq_len):
    m, d = x2.shape
    kern = functools.partial(_in_proj_qk_kernel, steps_per_seq=seq_len // TM_QK)
    return pl.pallas_call(
        kern,
        grid=(m // TM_QK,),
        in_specs=[
            pl.BlockSpec((TM_QK, d), lambda i: (i, 0)),
            pl.BlockSpec((1, d), lambda i: (0, 0)),
            pl.BlockSpec((d, 2 * ATTN_WIDTH), lambda i: (0, 0), pipeline_mode=pl.Buffered(1)),
            pl.BlockSpec((HEAD_ROWS, d), lambda i: (0, 0)),
            pl.BlockSpec((HEAD_ROWS, 1), lambda i: (0, 0)),
        ],
        out_specs=[
            pl.BlockSpec((TM_QK, 2 * ATTN_WIDTH), lambda i: (i, 0)),
            pl.BlockSpec((HEAD_ROWS, TM_QK), lambda i: (0, i)),
            pl.BlockSpec((TM_QK, d), lambda i: (i, 0)),
            pl.BlockSpec((None, 2 * ATTN_HEADS, HEAD_DIM), lambda i: (i, 0, 0)),
        ],
        out_shape=[
            jax.ShapeDtypeStruct((m, 2 * ATTN_WIDTH), jnp.bfloat16),
            jax.ShapeDtypeStruct((HEAD_ROWS, m), jnp.float32),
            jax.ShapeDtypeStruct((m, d), jnp.bfloat16),
            jax.ShapeDtypeStruct((m // TM_QK, 2 * ATTN_HEADS, HEAD_DIM), jnp.float32),
        ],
        scratch_shapes=[pltpu.VMEM((HEAD_ROWS, 128), jnp.float32)],
        compiler_params=pltpu.CompilerParams(
            dimension_semantics=("arbitrary",), vmem_limit_bytes=VMEM_LIMIT_BYTES),
        name="in_proj_qk",
    )(x2, g, w_main, w_f, b_f)


def _in_proj_lru_kernel(h_ref, wvz_ref, wl_ref, cw_ref, cb_ref, wg_ref, br_ref, bi_ref,
                        lam_ref, gl_ref, proj_ref, mix_ref, xs_ref, a_ref, b_ref, zl_ref, hc_ref,
                        *, steps_per_seq):
    i = pl.program_id(0)
    halo = SUBLANES_F32
    f32 = jnp.float32
    pw = LRU_WIDTH // LRU_PHASES

    @pl.when(i % steps_per_seq == 0)
    def _():
        xs_ref[0:halo, :] = jnp.zeros((halo, LRU_WIDTH), f32)
        hc_ref[...] = jnp.zeros_like(hc_ref)

    def project_x_lru(c0):
        xs_ref[halo:halo + TM_IN, c0:c0 + pw] = jnp.dot(
            h_ref[...], wl_ref[:, c0:c0 + pw], preferred_element_type=f32)

    def phase(j):
        c0 = j * pw
        cols = slice(c0, c0 + pw)
        if j + 1 < LRU_PHASES:
            project_x_lru(c0 + pw)
        vz = slice(2 * c0, 2 * (c0 + pw))
        proj_ref[:, vz] = jnp.dot(h_ref[...], wvz_ref[:, vz],
                                  preferred_element_type=f32).astype(proj_ref.dtype)
        zl_ref[:, cols] = jnp.dot(h_ref[...], wl_ref[:, LRU_WIDTH + c0:LRU_WIDTH + c0 + pw],
                                  preferred_element_type=f32)

        u = cb_ref[:, cols] + cw_ref[CONV_W - 1:CONV_W, cols] * xs_ref[halo:halo + TM_IN, cols]
        for k in range(CONV_W - 1):
            d = CONV_W - 1 - k
            u = u + cw_ref[k:k + 1, cols] * xs_ref[halo - d:halo - d + TM_IN, cols]
        xs_ref[0:halo, cols] = xs_ref[TM_IN:TM_IN + halo, cols]

        decay = LRU_C * _softplus(-lam_ref[:, cols])
        decay2 = -LOG2_E * decay
        ub = u.astype(jnp.bfloat16)
        for t in range(pw // LRU_BLOCK_W):
            lc = slice(t * LRU_BLOCK_W, (t + 1) * LRU_BLOCK_W)
            bc = slice(c0 + t * LRU_BLOCK_W, c0 + (t + 1) * LRU_BLOCK_W)
            gate = jnp.dot(ub[:, lc], wg_ref[c0 // LRU_BLOCK_W + t], preferred_element_type=f32)
            r = _sigmoid_of_double(gate[:, :LRU_BLOCK_W] + br_ref[:, bc])
            ig = _sigmoid_of_double(gate[:, LRU_BLOCK_W:] + bi_ref[:, bc])
            a = jnp.exp2(r * decay2[:, lc])
            a_ref[:, bc] = a
            one_minus_a2 = jnp.tanh(r * decay[:, lc]) * (1.0 + a * a)
            pos = one_minus_a2 > 0.0
            mult = jnp.where(pos, one_minus_a2 * lax.rsqrt(jnp.where(pos, one_minus_a2, 1.0)), 0.0)
            b_ref[:, bc] = mult * ig * u[:, lc]

        row = lax.broadcasted_iota(jnp.int32, (SUBLANES_F32, pw), 0)
        hprev = hc_ref[0:1, cols]
        for gidx in range(TM_IN // SUBLANES_F32):
            rows = slice(gidx * SUBLANES_F32, (gidx + 1) * SUBLANES_F32)
            av = a_ref[rows, cols]
            bv = b_ref[rows, cols]
            for d in (1, 2, 4):
                a_sh = jnp.where(row >= d, pltpu.roll(av, d, 0), 1.0)
                b_sh = jnp.where(row >= d, pltpu.roll(bv, d, 0), 0.0)
                bv = av * b_sh + bv
                av = av * a_sh
            hv = av * hprev + bv
            b_ref[rows, cols] = hv
            hprev = hv[SUBLANES_F32 - 1:SUBLANES_F32, :]
        hc_ref[0:1, cols] = hprev

    always = i >= 0
    project_x_lru(0)
    for j in range(LRU_PHASES):
        pl.when(always)(functools.partial(phase, j))

    hv = b_ref[...]
    z = zl_ref[...]
    y = hv * _rms_scale(hv) * gl_ref[...]
    mix_ref[...] = (y * _silu(z)).astype(mix_ref.dtype)


def _in_proj_lru(h, w_main, conv_w, conv_b, w_gate, b_rg, b_ig, lam, g_lru, seq_len):
    m, d = h.shape
    kern = functools.partial(_in_proj_lru_kernel, steps_per_seq=seq_len // TM_IN)
    vec = lambda: pl.BlockSpec((1, LRU_WIDTH), lambda i: (0, 0))
    wblk = lambda j: pl.BlockSpec((d, 2 * ATTN_WIDTH), lambda i: (0, j), pipeline_mode=pl.Buffered(1))
    return pl.pallas_call(
        kern,
        grid=(m // TM_IN,),
        in_specs=[
            pl.BlockSpec((TM_IN, d), lambda i: (i, 0)),
            wblk(1),
            wblk(2),
            pl.BlockSpec((CONV_W, LRU_WIDTH), lambda i: (0, 0)),
            vec(),
            pl.BlockSpec((LRU_BLOCKS, LRU_BLOCK_W, 2 * LRU_BLOCK_W), lambda i: (0, 0, 0)),
            vec(), vec(), vec(), vec(),
        ],
        out_specs=[
            pl.BlockSpec((TM_IN, 2 * ATTN_WIDTH), lambda i: (i, 0)),
            pl.BlockSpec((TM_IN, LRU_WIDTH), lambda i: (i, 0)),
        ],
        out_shape=[
            jax.ShapeDtypeStruct((m, 2 * ATTN_WIDTH), jnp.bfloat16),
            jax.ShapeDtypeStruct((m, LRU_WIDTH), jnp.bfloat16),
        ],
        scratch_shapes=[
            pltpu.VMEM((TM_IN + SUBLANES_F32, LRU_WIDTH), jnp.float32),
            pltpu.VMEM((TM_IN, LRU_WIDTH), jnp.float32),
            pltpu.VMEM((TM_IN, LRU_WIDTH), jnp.float32),
            pltpu.VMEM((TM_IN, LRU_WIDTH), jnp.float32),
            pltpu.VMEM((SUBLANES_F32, LRU_WIDTH), jnp.float32),
        ],
        compiler_params=pltpu.CompilerParams(
            dimension_semantics=("arbitrary",), vmem_limit_bytes=VMEM_LIMIT_BYTES),
        name="in_proj_lru",
    )(h, w_main, w_main, conv_w, conv_b, w_gate, b_rg, b_ig, lam, g_lru)


def _attn_kernel(q_ref, k_ref, v_ref, c_ref, bnd_ref, o_ref, vx_ref, s_ref, m_ref, acc_ref,
                 trips_ref):
    h = pl.program_id(1)
    nq = trips_ref.shape[0]
    lanes = HEAD_DIM
    sub = TQ // TK
    all_rows = (0, TQ)

    def scores(tile, c, rows):
        r0, n = rows
        q0 = pl.multiple_of(tile * TQ, TQ)
        k0 = pl.multiple_of(c * TK, TK)
        s = lax.dot_general(q_ref[pl.ds(q0 + r0, n), :], k_ref[pl.ds(k0, TK), :],
                            (((1,), (1,)), ((), ())), preferred_element_type=jnp.float32)
        return s - c_ref[pl.ds(h, 1), pl.ds(k0, TK)]

    vx_ref[:, 0:lanes] = v_ref[...]
    vx_ref[:, lanes:] = jnp.ones((vx_ref.shape[0], lanes), vx_ref.dtype)
    s_ref[0] = scores(0, 0, all_rows)

    chunk = lax.broadcasted_iota(jnp.int32, (nq, lanes), 1)
    n_left = lax.broadcasted_iota(jnp.int32, (nq, lanes), 0) * sub
    c_end = bnd_ref[0:1, :]
    k_n2 = bnd_ref[1:2, :]
    c_q = bnd_ref[SUBLANES_F32:SUBLANES_F32 + nq, :]
    q_n2 = bnd_ref[SUBLANES_F32 + nq:SUBLANES_F32 + 2 * nq, :]
    slack = (c_end - c_q) - SKIP_EXPONENT
    skippable = (slack > 0.0) & (slack * slack > 4.0 * q_n2 * k_n2)
    first_live = jnp.min(jnp.where(skippable | (chunk >= n_left), n_left, chunk),
                         axis=1, keepdims=True)
    live_pairs = n_left[:, 0:1] // 2 - first_live // 2
    for r in range(nq):
        trips_ref[r] = live_pairs[r, 0]

    def step(c, slot, rows, masked, first, next_tile, next_c, next_rows):
        r0, n = rows
        s = s_ref[slot, r0:r0 + n, :]
        s_ref[1 - slot, next_rows[0]:next_rows[0] + next_rows[1], :] = scores(
            next_tile, next_c, next_rows)
        if masked:
            row = lax.broadcasted_iota(jnp.int32, (TK, TK), 0)
            col = lax.broadcasted_iota(jnp.int32, (TK, TK), 1)
            tri = jnp.where(col <= row, s[0:TK, :], _NEG_BIG)
            s = tri if n == TK else jnp.concatenate([tri, s[TK:, :]], axis=0)
        tiles = [s[:, i * lanes:(i + 1) * lanes] for i in range(TK // lanes)]
        m_cur = jnp.max(functools.reduce(jnp.maximum, tiles), axis=1, keepdims=True)
        k0 = pl.multiple_of(c * TK, TK)
        if first:
            m_new = jnp.broadcast_to(m_cur, (n, lanes))
        else:
            m_prev = m_ref[r0:r0 + n, :]
            m_new = jnp.maximum(m_prev, m_cur)
            alpha = jnp.exp2(m_prev - m_new)
        p = jnp.concatenate([jnp.exp2(t - m_new) for t in tiles], axis=1).astype(jnp.bfloat16)
        pv = jnp.dot(p, vx_ref[pl.ds(k0, TK), :], preferred_element_type=jnp.float32)
        if first:
            acc_ref[r0:r0 + n, :] = pv
        else:
            acc_ref[r0:r0 + n, 0:lanes] = alpha * acc_ref[r0:r0 + n, 0:lanes] + pv[:, 0:lanes]
            acc_ref[r0:r0 + n, lanes:] = alpha * acc_ref[r0:r0 + n, lanes:] + pv[:, lanes:]
        m_ref[r0:r0 + n, :] = m_new

    def tile(qi, carry):
        n_off = qi * sub
        trips = trips_ref[qi]

        nxt_tile = jnp.minimum(qi + 1, nq - 1)
        for si in range(sub):
            if si + 1 < sub:
                step(n_off + si, si % 2, (si * TK, TQ - si * TK), True, si == 0,
                     qi, n_off + si + 1, ((si + 1) * TK, TQ - (si + 1) * TK))
            else:
                more = trips > 0
                step(n_off + si, si % 2, (si * TK, TQ - si * TK), True, si == 0,
                     jnp.where(more, qi, nxt_tile), jnp.where(more, n_off - 1, nxt_tile * sub),
                     all_rows)

        def body(i, carry):
            hi = n_off - 1 - 2 * i
            more = i + 1 < trips
            step(hi, 0, all_rows, False, False, qi, hi - 1, all_rows)
            step(hi - 1, 1, all_rows, False, False,
                 jnp.where(more, qi, nxt_tile), jnp.where(more, hi - 2, nxt_tile * sub), all_rows)
            return carry

        lax.fori_loop(0, trips, body, 0)
        o_ref[pl.ds(pl.multiple_of(qi * TQ, TQ), TQ), :] = (
            acc_ref[:, 0:lanes] / acc_ref[:, lanes:]).astype(o_ref.dtype)
        return carry

    lax.fori_loop(0, nq, tile, 0)


def _attention(proj_qk, proj_vz, c_t, norms, batch, seq_len):
    assert TQ % TK == 0 and (TQ // TK) % 2 == 0
    assert TQ % TM_QK == 0
    m = proj_qk.shape[0]
    nq = seq_len // TQ
    nch = seq_len // TK
    assert nch <= HEAD_DIM
    k_off = ATTN_WIDTH // HEAD_DIM
    c3 = c_t[:ATTN_HEADS].reshape(ATTN_HEADS, batch, seq_len)
    n2 = norms[:, :, 0].reshape(batch, nq, TQ // TM_QK, 2 * ATTN_HEADS)
    qn = n2[..., :ATTN_HEADS].max(axis=2).transpose(2, 0, 1)
    kn = n2[..., ATTN_HEADS:].max(axis=(1, 2)).T
    lanes_of = lambda a: jnp.broadcast_to(a[..., None], a.shape + (HEAD_DIM,))
    c_end = jnp.pad(c3[:, :, TK - 1::TK], ((0, 0), (0, 0), (0, HEAD_DIM - nch)))
    head_rows = jnp.stack([c_end, lanes_of(kn)], axis=2)
    head_rows = jnp.pad(head_rows, ((0, 0), (0, 0), (0, SUBLANES_F32 - 2), (0, 0)))
    bounds = jnp.concatenate([head_rows, lanes_of(c3[:, :, ::TQ]), lanes_of(qn)], axis=2)
    return pl.pallas_call(
        _attn_kernel,
        grid=(batch, ATTN_HEADS),
        in_specs=[
            pl.BlockSpec((seq_len, HEAD_DIM), lambda b, h: (b, h)),
            pl.BlockSpec((seq_len, HEAD_DIM), lambda b, h: (b, k_off + h)),
            pl.BlockSpec((seq_len, HEAD_DIM), lambda b, h: (b, h)),
            pl.BlockSpec((HEAD_ROWS, seq_len), lambda b, h: (0, b)),
            pl.BlockSpec((None, None, SUBLANES_F32 + 2 * nq, HEAD_DIM), lambda b, h: (h, b, 0, 0)),
        ],
        out_specs=pl.BlockSpec((seq_len, HEAD_DIM), lambda b, h: (b, h)),
        out_shape=jax.ShapeDtypeStruct((m, ATTN_WIDTH), jnp.bfloat16),
        scratch_shapes=[
            pltpu.VMEM((seq_len, 2 * HEAD_DIM), jnp.bfloat16),
            pltpu.VMEM((2, TQ, TK), jnp.float32),
            pltpu.VMEM((TQ, HEAD_DIM), jnp.float32),
            pltpu.VMEM((TQ, 2 * HEAD_DIM), jnp.float32),
            pltpu.SMEM((nq,), jnp.int32),
        ],
        compiler_params=pltpu.CompilerParams(
            dimension_semantics=("arbitrary", "arbitrary"),
            vmem_limit_bytes=VMEM_LIMIT_BYTES),
        name="fox_attn",
    )(proj_qk, proj_qk, proj_vz, c_t, bounds)


def _out_proj_kernel(x_ref, a_ref, z_ref, l_ref, ga_ref, w_ref, gf_ref, o_ref):
    a = a_ref[...].astype(jnp.float32)
    z = z_ref[...].astype(jnp.float32)
    ma = (a * _rms_scale(a) * ga_ref[...] * _silu(z)).astype(jnp.bfloat16)
    o = (x_ref[...]
         + jnp.dot(ma, w_ref[0:ATTN_WIDTH, :], preferred_element_type=jnp.float32)
         + jnp.dot(l_ref[...], w_ref[ATTN_WIDTH:, :], preferred_element_type=jnp.float32))
    o_ref[...] = o * _rms_scale(o) * gf_ref[...]


def _out_proj(x2, attn, proj_vz, mixed_lru, g_attn, w_out, g_final):
    m, d = x2.shape
    z_blk = 1
    return pl.pallas_call(
        _out_proj_kernel,
        grid=(m // TM_OUT,),
        in_specs=[
            pl.BlockSpec((TM_OUT, d), lambda i: (i, 0)),
            pl.BlockSpec((TM_OUT, ATTN_WIDTH), lambda i: (i, 0)),
            pl.BlockSpec((TM_OUT, ATTN_WIDTH), lambda i: (i, z_blk)),
            pl.BlockSpec((TM_OUT, LRU_WIDTH), lambda i: (i, 0)),
            pl.BlockSpec((1, ATTN_WIDTH), lambda i: (0, 0)),
            pl.BlockSpec((ATTN_WIDTH + LRU_WIDTH, d), lambda i: (0, 0), pipeline_mode=pl.Buffered(1)),
            pl.BlockSpec((1, d), lambda i: (0, 0)),
        ],
        out_specs=pl.BlockSpec((TM_OUT, d), lambda i: (i, 0)),
        out_shape=jax.ShapeDtypeStruct((m, d), jnp.float32),
        compiler_params=pltpu.CompilerParams(
            dimension_semantics=("arbitrary",), vmem_limit_bytes=VMEM_LIMIT_BYTES),
        name="out_proj",
    )(x2, attn, proj_vz, mixed_lru, g_attn, w_out, g_final)


def _pack_w_in_kernel(wt_ref, o_ref):
    f_lo = 3 * ATTN_WIDTH
    f_hi = f_lo + ATTN_HEADS
    o_ref[:, 0:f_lo] = wt_ref[0:f_lo, :].T.astype(o_ref.dtype)
    o_ref[:, f_lo:] = wt_ref[f_hi:, :].T.astype(o_ref.dtype)


def _pack_w_in(w0t):
    n_in, d = w0t.shape
    return pl.pallas_call(
        _pack_w_in_kernel,
        grid=(d // TR_PACK,),
        in_specs=[pl.BlockSpec((n_in, TR_PACK), lambda i: (0, i))],
        out_specs=pl.BlockSpec((TR_PACK, N_MAIN), lambda i: (i, 0)),
        out_shape=jax.ShapeDtypeStruct((d, N_MAIN), jnp.bfloat16),
        compiler_params=pltpu.CompilerParams(
            dimension_semantics=("arbitrary",), vmem_limit_bytes=VMEM_LIMIT_BYTES),
        name="pack_w_in",
    )(w0t)


def kernel(x, norm_g, w_in, b_f, conv_w, conv_b, w_rg, b_rg, w_ig, b_ig, lru_lambda,
           attn_norm_g, lru_norm_g, w_out, final_norm_g):
    batch, seq_len, d = x.shape
    assert norm_g.shape[0] == 1, "single-layer trunk"
    assert seq_len % TM_IN == 0 and seq_len % TM_QK == 0
    assert seq_len % TQ == 0 and seq_len % TM_OUT == 0
    f_lo = 3 * ATTN_WIDTH
    f_hi = f_lo + ATTN_HEADS
    bf16 = jnp.bfloat16

    w0t = w_in[0].T
    w_main = _pack_w_in(w0t)
    w_f = jnp.pad(w0t[f_lo:f_hi], ((0, HEAD_ROWS - ATTN_HEADS), (0, 0))).astype(bf16)
    bf_col = jnp.pad(b_f[0], (0, HEAD_ROWS - ATTN_HEADS)).reshape(HEAD_ROWS, 1)
    w_gate = (0.5 * jnp.concatenate([w_rg[0], w_ig[0]], axis=-1)).astype(bf16)
    row = lambda v: v.reshape(1, -1)

    x2 = x.reshape(batch * seq_len, d)
    g_in = row(norm_g[0])
    proj_qk, c_t, h, norms = _in_proj_qk(x2, g_in, w_main, w_f, bf_col, seq_len)
    proj_vz, mixed_lru = _in_proj_lru(h, w_main, conv_w[0], row(conv_b[0]), w_gate,
                                      row(0.5 * b_rg[0]), row(0.5 * b_ig[0]), row(lru_lambda[0]),
                                      row(lru_norm_g[0]), seq_len)
    attn = _attention(proj_qk, proj_vz, c_t, norms, batch, seq_len)
    out = _out_proj(x2, attn, proj_vz, mixed_lru, row(attn_norm_g[0]), w_out[0].astype(bf16),
                    row(final_norm_g))
    return out.reshape(batch, seq_len, d)
```
